```python
import math
import jax
import jax.numpy as jnp
from jax import lax
import numpy as np

D_MODEL = 1024
BATCH = 4
SEQ = 8192
DEPTH = 2
DEC_BATCH = 16
DEC_SEQ = 32
PAST_LEN = 1024

CHUNK = 64
EPS = 1e-6
M_HEADS = 4
M_DIM = D_MODEL
M_HEAD_DIM = M_DIM // M_HEADS
S_DIM = 2 * D_MODEL
S_HEAD_DIM = 64
S_HEADS = S_DIM // S_HEAD_DIM
S_GROUPS = 8
S_STATE = 128
HEADS_PER_GROUP = S_HEADS // S_GROUPS
CONV_W = 4
CONV_DIM = S_DIM + 2 * S_GROUPS * S_STATE
P_HEADS = 8
N_KEYS = 128
N_EXPERTS = N_KEYS * N_KEYS
P_KEY_DIM = 256
P_HALF = P_KEY_DIM // 2
P_TOPK = 16
P_BLOCK = 256
SPLIT_SIZES = (M_DIM, M_DIM, M_DIM, M_DIM, M_HEADS, M_HEADS, S_DIM, CONV_DIM, S_HEADS, D_MODEL, D_MODEL)
IN_DIM = 4 * M_DIM + 2 * M_HEADS + S_DIM + CONV_DIM + S_HEADS + 2 * D_MODEL

kernel_name = 'hybrid_mlstm_ssd_peer_stream'

F32 = jnp.float32


def rmsnorm(x, w):
    xf = x.astype(F32)
    y = xf * lax.rsqrt(jnp.mean(xf * xf, axis=-1, keepdims=True) + EPS)
    return (y * w.astype(F32)).astype(x.dtype)


def mlstm_chunk(carry, inp):
    c, n, m = carry
    q, k, v, ig, lf = inp
    L = q.shape[2]
    causal = jnp.tril(jnp.ones((L, L), dtype=bool))
    fcum = jnp.cumsum(lf, axis=-1)
    logw = jnp.where(causal, fcum[..., :, None] - fcum[..., None, :] + ig[..., None, :], -jnp.inf)
    log_prev = fcum + m[..., None]
    m_t = jnp.maximum(log_prev, jnp.max(logw, axis=-1))
    a_prev = jnp.exp(log_prev - m_t)
    s = jnp.einsum('bhtd,bhsd->bhts', q, k) * jnp.exp(logw - m_t[..., None])
    num = jnp.einsum('bhts,bhse->bhte', s, v) + a_prev[..., None] * jnp.einsum('bhtd,bhde->bhte', q, c)
    den = jnp.sum(s, axis=-1) + a_prev * jnp.einsum('bhtd,bhd->bht', q, n)
    h = num / jnp.maximum(jnp.abs(den), jnp.exp(-m_t))[..., None]
    f_end = fcum[..., -1]
    m_new = m_t[..., -1]
    w_end = jnp.exp(f_end[..., None] - fcum + ig - m_new[..., None])
    a_end = jnp.exp(f_end + m - m_new)
    c_new = a_end[..., None, None] * c + jnp.einsum('bhs,bhsd,bhse->bhde', w_end, k, v)
    n_new = a_end[..., None] * n + jnp.einsum('bhs,bhsd->bhd', w_end, k)
    return (c_new, n_new, m_new), h


def mlstm_mixer(q, k, v, ig_pre, fg_pre, c0, n0, m0):
    b, s, _ = q.shape
    L = min(CHUNK, s)
    nc = s // L

    def heads(t):
        return t.astype(F32).reshape(b, nc, L, M_HEADS, M_HEAD_DIM).transpose(1, 0, 3, 2, 4)

    def gate(g):
        return g.astype(F32).reshape(b, nc, L, M_HEADS).transpose(1, 0, 3, 2)

    xs = (heads(q), heads(k) * (M_HEAD_DIM ** -0.5), heads(v), gate(ig_pre), jax.nn.log_sigmoid(gate(fg_pre)))
    (c, n, m), h = lax.scan(mlstm_chunk, (c0.astype(F32), n0.astype(F32), m0.astype(F32)), xs)
    h = h.transpose(1, 0, 3, 2, 4).reshape(b, s, M_HEADS, M_HEAD_DIM)
    return h, c, n, m


def ssd_chunk(h, inp):
    x, dt, la, bm, cm = inp
    L = x.shape[1]
    causal = jnp.tril(jnp.ones((L, L), dtype=bool))[None, :, :, None]
    cum = jnp.cumsum(la, axis=1)
    decay = jnp.exp(jnp.where(causal, cum[:, :, None, :] - cum[:, None, :, :], -jnp.inf))
    cb = jnp.repeat(jnp.einsum('btgn,bsgn->btsg', cm, bm), HEADS_PER_GROUP, axis=-1)
    y = jnp.einsum('btsh,bshp->bthp', cb * decay * dt[:, None, :, :], x)
    ch = jnp.repeat(cm, HEADS_PER_GROUP, axis=2)
    bh = jnp.repeat(bm, HEADS_PER_GROUP, axis=2)
    y = y + jnp.exp(cum)[..., None] * jnp.einsum('bthn,bhpn->bthp', ch, h)
    w_end = jnp.exp(cum[:, -1:, :] - cum) * dt
    h_new = jnp.exp(cum[:, -1, :])[:, :, None, None] * h + jnp.einsum('bsh,bshn,bshp->bhpn', w_end, bh, x)
    return h_new, y


def mamba_mixer(z, xbc, dt_pre, conv0, ssm0, conv_w, conv_b, dt_bias, a_log, d_skip, norm_w):
    b, s, _ = xbc.shape
    xpad = jnp.concatenate([conv0.astype(xbc.dtype), xbc], axis=1)
    conv_new = xpad[:, s:]
    conv = conv_b + sum(xpad[:, i:i + s] * conv_w[i] for i in range(CONV_W))
    xbc = jax.nn.silu(conv.astype(F32))
    xs, bm, cm = jnp.split(xbc, [S_DIM, S_DIM + S_GROUPS * S_STATE], axis=-1)
    dt = jax.nn.softplus(dt_pre.astype(F32) + dt_bias.astype(F32))
    la = dt * (-jnp.exp(a_log.astype(F32)))
    L = min(CHUNK, s)
    nc = s // L

    def blk(t, *tail):
        return t.reshape((b, nc, L) + tail).swapaxes(0, 1)

    xh = xs.reshape(b, s, S_HEADS, S_HEAD_DIM)
    inp = (blk(xh, S_HEADS, S_HEAD_DIM), blk(dt, S_HEADS), blk(la, S_HEADS),
           blk(bm, S_GROUPS, S_STATE), blk(cm, S_GROUPS, S_STATE))
    ssm_new, y = lax.scan(ssd_chunk, ssm0.astype(F32), inp)
    y = y.swapaxes(0, 1).reshape(b, s, S_HEADS, S_HEAD_DIM) + d_skip.astype(F32)[:, None] * xh
    y = y.reshape(b, s, S_DIM) * jax.nn.silu(z.astype(F32))
    return rmsnorm(y, norm_w), ssm_new, conv_new


def peer(x, wq, keys, u, v):
    shp = x.shape
    xt = x.reshape(-1, D_MODEL)
    t = xt.shape[0]
    nb = -(-t // P_BLOCK)
    xt = jnp.pad(xt, ((0, nb * P_BLOCK - t), (0, 0))).reshape(nb, P_BLOCK, D_MODEL)

    def block(xb):
        q = (xb @ wq).astype(F32).reshape(P_BLOCK, P_HEADS, 2, P_HALF)
        sc = jnp.einsum('thcd,hckd->thck', q, keys.astype(F32))
        sv, si = lax.top_k(sc, P_TOPK)
        cand = (sv[:, :, 0, :, None] + sv[:, :, 1, None, :]).reshape(P_BLOCK, P_HEADS, P_TOPK * P_TOPK)
        cidx = (si[:, :, 0, :, None] * N_KEYS + si[:, :, 1, None, :]).reshape(P_BLOCK, P_HEADS, P_TOPK * P_TOPK)
        cv, ci = lax.top_k(cand, P_TOPK)
        eidx = jnp.take_along_axis(cidx, ci, axis=-1)
        g = jax.nn.softmax(cv, axis=-1)
        act = jax.nn.gelu(jnp.einsum('thkd,td->thk', u[eidx], xb).astype(F32))
        return jnp.einsum('thk,thkd->td', (g * act).astype(xb.dtype), v[eidx])

    out = lax.map(block, xt)
    return out.reshape(-1, D_MODEL)[:t].reshape(shp)


def trunk(x, c0, n0, m0, ssm0, conv0, weights):
    (norm1_w, w_in, b_igate, b_fgate, mlstm_norm_w, conv_w, conv_b, dt_bias, a_log, d_skip,
     ssm_norm_w, w_a, w_b, w_out, norm2_w, peer_wq, peer_keys, peer_u, peer_v, final_norm_w) = weights
    split_at = np.cumsum(SPLIT_SIZES)[:-1].tolist()
    new_c, new_n, new_m, new_ssm, new_conv = [], [], [], [], []
    for l in range(DEPTH):
        h = rmsnorm(x, norm1_w[l])
        proj = h @ w_in[l]
        q, k, v, og, ig, fg, z, xbc, dtp, ga, gb = jnp.split(proj, split_at, axis=-1)
        hm, c, n, m = mlstm_mixer(q, k, v, ig + b_igate[l], fg + b_fgate[l], c0[l], n0[l], m0[l])
        hm = rmsnorm(hm, mlstm_norm_w[l].reshape(M_HEADS, M_HEAD_DIM)).reshape(x.shape[:2] + (M_DIM,))
        hm = (hm * jax.nn.sigmoid(og.astype(F32))).astype(x.dtype)
        ys, ssm, cv = mamba_mixer(z, xbc, dtp, conv0[l], ssm0[l], conv_w[l], conv_b[l], dt_bias[l],
                                  a_log[l], d_skip[l], ssm_norm_w[l])
        mix = jax.nn.sigmoid(ga) * (hm @ w_a[l]) + jax.nn.sigmoid(gb) * (ys.astype(x.dtype) @ w_b[l])
        x = x + mix @ w_out[l]
        x = x + peer(rmsnorm(x, norm2_w[l]), peer_wq[l], peer_keys[l], peer_u[l], peer_v[l])
        new_c.append(c)
        new_n.append(n)
        new_m.append(m)
        new_ssm.append(ssm)
        new_conv.append(cv)
    y = rmsnorm(x, final_norm_w)
    return y, jnp.stack(new_c), jnp.stack(new_n), jnp.stack(new_m), jnp.stack(new_ssm), jnp.stack(new_conv)


def setup_inputs(seed: int = 0) -> dict:
    key = jax.random.key(seed)
    ks = iter(jax.random.split(key, 32))

    def nrm(shape, scale):
        return jax.random.normal(next(ks), shape, F32) * scale

    def gain(shape):
        return 1.0 + nrm(shape, 0.02)

    x_prompt = nrm((BATCH, SEQ, D_MODEL), 1.0)
    x_sample = nrm((DEC_BATCH, DEC_SEQ, D_MODEL), 1.0)
    state_mlstm_C = nrm((DEPTH, DEC_BATCH, M_HEADS, M_HEAD_DIM, M_HEAD_DIM), 0.1)
    state_mlstm_n = nrm((DEPTH, DEC_BATCH, M_HEADS, M_HEAD_DIM), 0.5)
    state_mlstm_m = jax.random.uniform(next(ks), (DEPTH, DEC_BATCH, M_HEADS), F32, 0.0, 2.0)
    state_ssm = nrm((DEPTH, DEC_BATCH, S_HEADS, S_HEAD_DIM, S_STATE), 0.1)
    cache_conv = nrm((DEPTH, DEC_BATCH, CONV_W - 1, CONV_DIM), 1.0)
    norm1_w = gain((DEPTH, D_MODEL))
    w_in = nrm((DEPTH, D_MODEL, IN_DIM), D_MODEL ** -0.5)
    b_igate = nrm((DEPTH, M_HEADS), 0.1)
    b_fgate = jnp.linspace(3.0, 6.0, M_HEADS, dtype=F32)[None, :] + nrm((DEPTH, M_HEADS), 0.1)
    mlstm_norm_w = gain((DEPTH, M_DIM))
    conv_w = nrm((DEPTH, CONV_W, CONV_DIM), CONV_W ** -0.5)
    conv_b = nrm((DEPTH, CONV_DIM), 0.02)
    dt0 = jnp.exp(jax.random.uniform(next(ks), (DEPTH, S_HEADS), F32, math.log(1e-3), math.log(1e-1)))
    dt_bias = dt0 + jnp.log(-jnp.expm1(-dt0))
    a_log = jnp.log(jax.random.uniform(next(ks), (DEPTH, S_HEADS), F32, 1.0, 16.0))
    d_skip = gain((DEPTH, S_HEADS))
    ssm_norm_w = gain((DEPTH, S_DIM))
    w_a = nrm((DEPTH, M_DIM, D_MODEL), M_DIM ** -0.5)
    w_b = nrm((DEPTH, S_DIM, D_MODEL), S_DIM ** -0.5)
    w_out = nrm((DEPTH, D_MODEL, D_MODEL), D_MODEL ** -0.5)
    norm2_w = gain((DEPTH, D_MODEL))
    peer_wq = nrm((DEPTH, D_MODEL, P_HEADS * P_KEY_DIM), D_MODEL ** -0.5)
    peer_keys = nrm((DEPTH, P_HEADS, 2, N_KEYS, P_HALF), P_HALF ** -0.5)
    peer_u = nrm((DEPTH, N_EXPERTS, D_MODEL), D_MODEL ** -0.5)
    peer_v = nrm((DEPTH, N_EXPERTS, D_MODEL), P_HEADS ** -0.5)
    final_norm_w = gain((D_MODEL,))
    return {'x_prompt': x_prompt, 'x_sample': x_sample,
            'state_mlstm_C': state_mlstm_C, 'state_mlstm_n': state_mlstm_n, 'state_mlstm_m': state_mlstm_m,
            'state_ssm': state_ssm, 'cache_conv': cache_conv,
            'norm1_w': norm1_w, 'w_in': w_in, 'b_igate': b_igate, 'b_fgate': b_fgate,
            'mlstm_norm_w': mlstm_norm_w, 'conv_w': conv_w, 'conv_b': conv_b, 'dt_bias': dt_bias,
            'a_log': a_log, 'd_skip': d_skip, 'ssm_norm_w': ssm_norm_w, 'w_a': w_a, 'w_b': w_b,
            'w_out': w_out, 'norm2_w': norm2_w, 'peer_wq': peer_wq, 'peer_keys': peer_keys,
            'peer_u': peer_u, 'peer_v': peer_v, 'final_norm_w': final_norm_w}


def reference(x_prompt, x_sample, state_mlstm_C, state_mlstm_n, state_mlstm_m, state_ssm, cache_conv,
              norm1_w, w_in, b_igate, b_fgate, mlstm_norm_w, conv_w, conv_b, dt_bias, a_log, d_skip,
              ssm_norm_w, w_a, w_b, w_out, norm2_w, peer_wq, peer_keys, peer_u, peer_v, final_norm_w):
    weights = (norm1_w, w_in, b_igate, b_fgate, mlstm_norm_w, conv_w, conv_b, dt_bias, a_log, d_skip,
               ssm_norm_w, w_a, w_b, w_out, norm2_w, peer_wq, peer_keys, peer_u, peer_v, final_norm_w)
    bp = x_prompt.shape[0]
    zc = jnp.zeros((DEPTH, bp, M_HEADS, M_HEAD_DIM, M_HEAD_DIM), F32)
    zn = jnp.zeros((DEPTH, bp, M_HEADS, M_HEAD_DIM), F32)
    zm = jnp.zeros((DEPTH, bp, M_HEADS), F32)
    zs = jnp.zeros((DEPTH, bp, S_HEADS, S_HEAD_DIM, S_STATE), F32)
    zconv = jnp.zeros((DEPTH, bp, CONV_W - 1, CONV_DIM), x_prompt.dtype)
    y_prompt, pc, pn, pm, pssm, pconv = trunk(x_prompt, zc, zn, zm, zs, zconv, weights)
    y_sample, sc, sn, sm, sssm, sconv = trunk(x_sample, state_mlstm_C, state_mlstm_n, state_mlstm_m,
                                              state_ssm, cache_conv, weights)
    return (y_prompt, y_sample, pc, pn, pm, pssm, pconv, sc, sn, sm, sssm, sconv)
```

```python
import functools

import jax
import jax.numpy as jnp
from jax import lax
from jax.experimental import pallas as pl
from jax.experimental.pallas import tpu as pltpu

F32 = jnp.float32
BF16 = jnp.bfloat16
EPS = 1e-6
CONV_W = 4
PEER_TOPK = 16
LANES = 128
SUBLANES = 8
SMALL_W = 128
VMEM_LIMIT = 56 * 1024 * 1024
HIGHEST = lax.Precision.HIGHEST
NT_DIMS = (((1,), (1,)), ((), ()))


def _pick(n, prefs):
    for p in prefs:
        if n % p == 0:
            return p
    return n


def _softplus(x):
    return jnp.maximum(x, 0.0) + jnp.log1p(jnp.exp(-jnp.abs(x)))


def _log_sigmoid(x):
    return jnp.minimum(x, 0.0) - jnp.log1p(jnp.exp(-jnp.abs(x)))


def _rms(x, w):
    return x * lax.rsqrt(jnp.mean(x * x, axis=-1, keepdims=True) + EPS) * w


def _params(sem):
    return pltpu.CompilerParams(dimension_semantics=sem, vmem_limit_bytes=VMEM_LIMIT)


def _in_proj_kernel(x_ref, nw_ref, wbig_ref, wsmall_ref, bsmall_ref, big_ref, small_ref, h_sc):
    @pl.when(pl.program_id(1) == 0)
    def _():
        hb = _rms(x_ref[...], nw_ref[...]).astype(BF16)
        h_sc[...] = hb
        small_ref[...] = jnp.dot(hb, wsmall_ref[...], preferred_element_type=F32) + bsmall_ref[...]

    big_ref[...] = jnp.dot(h_sc[...], wbig_ref[...], preferred_element_type=F32).astype(BF16)


def _in_proj(x2d, nw, w_big, w_small, b_small):
    t, d = x2d.shape
    nbig = w_big.shape[1]
    tm = _pick(t, (1024, 512, 256, 128))
    tn = _pick(nbig, (2048, 1536, 1024, 512, 256, 128))
    return pl.pallas_call(
        _in_proj_kernel,
        grid=(t // tm, nbig // tn),
        in_specs=[
            pl.BlockSpec((tm, d), lambda i, j: (i, 0)),
            pl.BlockSpec((1, d), lambda i, j: (0, 0)),
            pl.BlockSpec((d, tn), lambda i, j: (0, j)),
            pl.BlockSpec((d, SMALL_W), lambda i, j: (0, 0)),
            pl.BlockSpec((1, SMALL_W), lambda i, j: (0, 0)),
        ],
        out_specs=[
            pl.BlockSpec((tm, tn), lambda i, j: (i, j)),
            pl.BlockSpec((tm, SMALL_W), lambda i, j: (i, 0)),
        ],
        out_shape=[jax.ShapeDtypeStruct((t, nbig), BF16), jax.ShapeDtypeStruct((t, SMALL_W), F32)],
        scratch_shapes=[pltpu.VMEM((tm, d), BF16)],
        compiler_params=_params(("parallel", "arbitrary")),
        name="in_proj",
    )(x2d, nw, w_big, w_small, b_small)


def _mlstm_kernel(q_ref, k_ref, v_ref, og_ref, g_ref, c0_ref, n0_ref, m0_ref, nw_ref,
                  h_ref, c_ref, n_ref, m_ref, c_sc, n_sc, m_sc, *, n_heads, scale):
    hh = pl.program_id(1)
    ci = pl.program_id(2)
    lc = q_ref.shape[1]

    @pl.when(ci == 0)
    def _():
        c_sc[...] = c0_ref[0, 0]
        n_sc[...] = n0_ref[0, 0]
        m_sc[...] = m0_ref[0, 0]

    ig = g_ref[0, pl.ds(hh, 1), :]
    lf = _log_sigmoid(g_ref[0, pl.ds(n_heads + hh, 1), :])
    row = lax.broadcasted_iota(jnp.int32, (lc, lc), 0)
    col = lax.broadcasted_iota(jnp.int32, (lc, lc), 1)
    causal = col <= row
    lf8 = jnp.broadcast_to(lf, (SUBLANES, lc))
    fcum_col = lax.dot_general(causal.astype(F32), lf8, NT_DIMS, precision=HIGHEST,
                               preferred_element_type=F32)[:, 0:1]
    fcum_row = jnp.dot(lf8, (row <= col).astype(F32), precision=HIGHEST,
                       preferred_element_type=F32)[0:1, :]

    m_prev = m_sc[...]
    logw = jnp.where(causal, fcum_col - fcum_row + ig, -jnp.inf)
    log_prev = fcum_col + m_prev
    m_t = jnp.maximum(log_prev, jnp.max(logw, axis=1, keepdims=True))
    a_prev = jnp.exp(log_prev - m_t)
    decay = jnp.exp(logw - m_t)

    q = q_ref[0]
    k = k_ref[0]
    v = v_ref[0]
    s = lax.dot_general(q, k, NT_DIMS, preferred_element_type=F32) * scale * decay
    num = (jnp.dot(s.astype(BF16), v, preferred_element_type=F32)
           + a_prev * jnp.dot(q, c_sc[...].astype(BF16), preferred_element_type=F32))
    qn = jnp.sum(q.astype(F32) * n_sc[...], axis=1, keepdims=True)
    den = jnp.sum(s, axis=1, keepdims=True) + a_prev * qn
    hout = num / jnp.maximum(jnp.abs(den), jnp.exp(-m_t))
    y = _rms(hout, nw_ref[0]) * jax.nn.sigmoid(og_ref[0].astype(F32))
    h_ref[0] = y.astype(BF16)

    f_end = fcum_row[:, lc - 1:lc]
    m_new = m_t[lc - 1:lc, :]
    w_row = jnp.exp(f_end - fcum_row + ig - m_new) * scale
    a_end = jnp.exp(f_end + m_prev - m_new)
    kw = (k.astype(F32).T * w_row).astype(BF16)
    c_sc[...] = a_end * c_sc[...] + jnp.dot(kw, v, preferred_element_type=F32)
    w8 = jnp.broadcast_to(w_row, (SUBLANES, lc)).astype(BF16)
    n_sc[...] = a_end * n_sc[...] + jnp.dot(w8, k, preferred_element_type=F32)[0:1, :]
    m_sc[...] = m_new

    @pl.when(ci == pl.num_programs(2) - 1)
    def _():
        c_ref[0, 0] = c_sc[...]
        n_ref[0, 0] = n_sc[...]
        m_ref[0, 0] = m_sc[...]


def _mlstm(big3, small_t, c0, n0, m0, norm_w):
    b, s, _ = big3.shape
    _, nh, dh, _ = c0.shape
    lc = _pick(s, (256, 128, 64, 32, 16))
    kern = functools.partial(_mlstm_kernel, n_heads=nh, scale=float(dh) ** -0.5)
    blk = lambda off: pl.BlockSpec((1, lc, dh), lambda bi, hi, ci: (bi, ci, off * nh + hi))
    st4 = lambda r, c: pl.BlockSpec((1, 1, r, c), lambda bi, hi, ci: (bi, hi, 0, 0))
    return pl.pallas_call(
        kern,
        grid=(b, nh, s // lc),
        in_specs=[blk(0), blk(1), blk(2), blk(3),
                  pl.BlockSpec((1, SUBLANES, lc), lambda bi, hi, ci: (bi, 0, ci)),
                  st4(dh, dh), st4(1, dh), st4(1, 1),
                  pl.BlockSpec((1, 1, dh), lambda bi, hi, ci: (hi, 0, 0))],
        out_specs=[pl.BlockSpec((1, lc, dh), lambda bi, hi, ci: (bi, ci, hi)),
                   st4(dh, dh), st4(1, dh), st4(1, 1)],
        out_shape=[jax.ShapeDtypeStruct((b, s, nh * dh), BF16),
                   jax.ShapeDtypeStruct((b, nh, dh, dh), F32),
                   jax.ShapeDtypeStruct((b, nh, 1, dh), F32),
                   jax.ShapeDtypeStruct((b, nh, 1, 1), F32)],
        scratch_shapes=[pltpu.VMEM((dh, dh), F32), pltpu.VMEM((1, dh), F32), pltpu.VMEM((1, 1), F32)],
        compiler_params=_params(("parallel", "parallel", "arbitrary")),
        name="mlstm",
    )(big3, big3, big3, big3, small_t, c0, n0.reshape(b, nh, 1, dh), m0.reshape(b, nh, 1, 1),
      norm_w.reshape(nh, 1, dh))


def _ssd_kernel(xbc_ref, z_ref, sm_ref, smt_ref, conv0_ref, ssm0_ref, cw_ref, cb_ref,
                alr_ref, alc_ref, dsk_ref, nw_ref, ex_ref,
                ys_ref, ssm_ref, conv_ref, xpad, ht_sc,
                *, dt_off, n_sh, n_groups, n_state, s_dim):
    ci = pl.program_id(1)
    lc = xbc_ref.shape[1]
    gw = s_dim // n_groups
    hpg = n_sh // n_groups
    pdim = gw // hpg
    pad = SUBLANES

    @pl.when(ci == 0)
    def _():
        xpad[pad - (CONV_W - 1):pad, :] = conv0_ref[0]
        for g in range(n_groups):
            ht_sc[g] = ssm0_ref[0, g].T

    xpad[pad:pad + lc, :] = xbc_ref[0].astype(F32)
    conv = cb_ref[...]
    for i in range(CONV_W):
        conv = conv + xpad[pad - (CONV_W - 1) + i:pad - (CONV_W - 1) + i + lc, :] * cw_ref[i:i + 1, :]
    tail = xpad[pad + lc - (CONV_W - 1):pad + lc, :]
    xpad[pad - (CONV_W - 1):pad, :] = tail
    act = conv * jax.nn.sigmoid(conv)
    xs = act[:, :s_dim]
    bm = act[:, s_dim:s_dim + n_groups * n_state]
    cm = act[:, s_dim + n_groups * n_state:]

    dt_col = _softplus(sm_ref[0][:, dt_off:dt_off + n_sh])
    dt_row = _softplus(smt_ref[0][dt_off:dt_off + n_sh, :])
    la_col = dt_col * (-jnp.exp(alr_ref[...]))
    la_row = dt_row * (-jnp.exp(alc_ref[...]))
    row = lax.broadcasted_iota(jnp.int32, (lc, lc), 0)
    col = lax.broadcasted_iota(jnp.int32, (lc, lc), 1)
    causal = col <= row
    cum_col = jnp.dot(causal.astype(F32), la_col, precision=HIGHEST, preferred_element_type=F32)
    cum_row = jnp.dot(la_row, (row <= col).astype(F32), precision=HIGHEST, preferred_element_type=F32)
    w_end = jnp.exp(cum_col[lc - 1:lc, :] - cum_col) * dt_col
    ex = ex_ref[...]
    ecum_x = jnp.dot(jnp.exp(cum_col), ex, precision=HIGHEST, preferred_element_type=F32)
    wend_x = jnp.dot(w_end, ex, precision=HIGHEST, preferred_element_type=F32)
    xw = (xs * wend_x).astype(BF16)
    xs_b = xs.astype(BF16)
    head_of_lane = lax.div(lax.broadcasted_iota(jnp.int32, (1, gw), 1), pdim)

    parts = []
    for g in range(n_groups):
        bm_g = bm[:, g * n_state:(g + 1) * n_state]
        cm_b = cm[:, g * n_state:(g + 1) * n_state].astype(BF16)
        cb = lax.dot_general(cm_b, bm_g.astype(BF16), NT_DIMS, preferred_element_type=F32)
        xg = xs_b[:, g * gw:(g + 1) * gw]
        yg = (jnp.dot(cm_b, ht_sc[g].astype(BF16), preferred_element_type=F32)
              * ecum_x[:, g * gw:(g + 1) * gw])
        for kk in range(hpg):
            hd = g * hpg + kk
            dec = jnp.exp(jnp.where(causal, cum_col[:, hd:hd + 1] - cum_row[hd:hd + 1, :], -jnp.inf))
            wm = (cb * dec * dt_row[hd:hd + 1, :]).astype(BF16)
            xm = jnp.where(head_of_lane == kk, xg, jnp.zeros_like(xg))
            yg = yg + jnp.dot(wm, xm, preferred_element_type=F32)
        ht_sc[g] = (ecum_x[lc - 1:lc, g * gw:(g + 1) * gw] * ht_sc[g]
                    + jnp.dot(bm_g.T.astype(BF16), xw[:, g * gw:(g + 1) * gw], preferred_element_type=F32))
        parts.append(yg)
    y = jnp.concatenate(parts, axis=1) + dsk_ref[...] * xs
    zf = z_ref[0].astype(F32)
    y = y * (zf * jax.nn.sigmoid(zf))
    ys_ref[0] = _rms(y, nw_ref[...]).astype(BF16)

    @pl.when(ci == pl.num_programs(1) - 1)
    def _():
        conv_ref[0] = tail
        for g in range(n_groups):
            ssm_ref[0, g] = ht_sc[g].T


def _ssd(big3, small3, small_t, conv0, ssm0, conv_w, conv_b, a_log, d_skip, norm_w, *, m_dim, n_mh):
    b, s, _ = big3.shape
    _, n_sh, pdim, n_state = ssm0.shape
    cd = conv0.shape[2]
    s_dim = n_sh * pdim
    n_groups = (cd - s_dim) // (2 * n_state)
    gw = s_dim // n_groups
    lc = _pick(s, (256, 128, 64, 32, 16))
    xbc_blk = (4 * m_dim) // cd
    z_blk = (4 * m_dim + cd) // s_dim
    assert xbc_blk * cd == 4 * m_dim and z_blk * s_dim == 4 * m_dim + cd
    kern = functools.partial(_ssd_kernel, dt_off=2 * n_mh, n_sh=n_sh, n_groups=n_groups,
                             n_state=n_state, s_dim=s_dim)
    full2 = lambda r, c: pl.BlockSpec((r, c), lambda bi, ci: (0, 0))
    expander = jnp.repeat(jnp.eye(n_sh, dtype=F32), pdim, axis=1)
    ys, ssm, conv = pl.pallas_call(
        kern,
        grid=(b, s // lc),
        in_specs=[pl.BlockSpec((1, lc, cd), lambda bi, ci: (bi, ci, xbc_blk)),
                  pl.BlockSpec((1, lc, s_dim), lambda bi, ci: (bi, ci, z_blk)),
                  pl.BlockSpec((1, lc, SMALL_W), lambda bi, ci: (bi, ci, 0)),
                  pl.BlockSpec((1, SMALL_W, lc), lambda bi, ci: (bi, 0, ci)),
                  pl.BlockSpec((1, CONV_W - 1, cd), lambda bi, ci: (bi, 0, 0)),
                  pl.BlockSpec((1, n_groups, gw, n_state), lambda bi, ci: (bi, 0, 0, 0)),
                  full2(CONV_W, cd), full2(1, cd), full2(1, n_sh), full2(n_sh, 1),
                  full2(1, s_dim), full2(1, s_dim), full2(n_sh, s_dim)],
        out_specs=[pl.BlockSpec((1, lc, s_dim), lambda bi, ci: (bi, ci, 0)),
                   pl.BlockSpec((1, n_groups, gw, n_state), lambda bi, ci: (bi, 0, 0, 0)),
                   pl.BlockSpec((1, CONV_W - 1, cd), lambda bi, ci: (bi, 0, 0))],
        out_shape=[jax.ShapeDtypeStruct((b, s, s_dim), BF16),
                   jax.ShapeDtypeStruct((b, n_groups, gw, n_state), F32),
                   jax.ShapeDtypeStruct((b, CONV_W - 1, cd), F32)],
        scratch_shapes=[pltpu.VMEM((lc + SUBLANES, cd), F32), pltpu.VMEM((n_groups, n_state, gw), F32)],
        compiler_params=_params(("parallel", "arbitrary")),
        name="ssd",
    )(big3, big3, small3, small_t, conv0, ssm0.reshape(b, n_groups, gw, n_state), conv_w,
      conv_b.reshape(1, cd), a_log.reshape(1, n_sh), a_log.reshape(n_sh, 1),
      jnp.repeat(d_skip, pdim).reshape(1, s_dim), norm_w.reshape(1, s_dim), expander)
    return ys, ssm.reshape(b, n_sh, pdim, n_state), conv


def _out_kernel(x_ref, hm_ref, ys_ref, ga_ref, gb_ref, wa_ref, wb_ref, wo_ref, n2_ref, xo_ref, xn_ref):
    a = jnp.dot(hm_ref[...], wa_ref[...], preferred_element_type=F32)
    bb = jnp.dot(ys_ref[...], wb_ref[...], preferred_element_type=F32)
    mix = jax.nn.sigmoid(ga_ref[...].astype(F32)) * a + jax.nn.sigmoid(gb_ref[...].astype(F32)) * bb
    xo = x_ref[...] + jnp.dot(mix.astype(BF16), wo_ref[...], preferred_element_type=F32)
    xo_ref[...] = xo
    xn_ref[...] = _rms(xo, n2_ref[...]).astype(BF16)


def _out_proj(x2d, hm2, ys2, big2, w_a, w_b, w_out, norm2_w, ga_blk):
    t, d = x2d.shape
    tm = _pick(t, (512, 256, 128))
    full = lambda a: pl.BlockSpec(a.shape, lambda i: (0, 0))
    rows = lambda w: pl.BlockSpec((tm, w), lambda i: (i, 0))
    return pl.pallas_call(
        _out_kernel,
        grid=(t // tm,),
        in_specs=[rows(d), rows(hm2.shape[1]), rows(ys2.shape[1]),
                  pl.BlockSpec((tm, d), lambda i: (i, ga_blk)),
                  pl.BlockSpec((tm, d), lambda i: (i, ga_blk + 1)),
                  full(w_a), full(w_b), full(w_out), full(norm2_w)],
        out_specs=[rows(d), rows(d)],
        out_shape=[jax.ShapeDtypeStruct((t, d), F32), jax.ShapeDtypeStruct((t, d), BF16)],
        compiler_params=_params(("parallel",)),
        name="out_proj",
    )(x2d, hm2, ys2, big2, big2, w_a, w_b, w_out, norm2_w)


N_TOP = PEER_TOPK + 1
TOP_ROWS = 24


def _topk_rows(s_ref, dst_ref, off):
    nk = s_ref.shape[0]
    cur = s_ref[:, pl.ds(off, LANES)]
    kid = lax.broadcasted_iota(jnp.int32, (nk, LANES), 0)
    for r in range(N_TOP):
        m = jnp.max(cur, axis=0, keepdims=True)
        dst_ref[r:r + 1, pl.ds(off, LANES)] = m
        first = jnp.min(jnp.where(cur == m, kid, nk), axis=0, keepdims=True)
        cur = jnp.where(kid == first, -jnp.inf, cur)


def _pair_threshold(a, b):
    rid = lax.broadcasted_iota(jnp.int32, (TOP_ROWS, LANES), 0)
    small_l = 4
    bm = jnp.where(rid >= small_l, b, -jnp.inf)
    pieces = [a + b[0:1]]
    for l in range(1, small_l):
        pieces.append(a[0:SUBLANES] + b[l:l + 1])
    pieces.append(bm + a[0:1])
    for kk in range(1, 3):
        pieces.append(bm[0:SUBLANES] + a[kk:kk + 1])
    cand = jnp.concatenate(pieces, axis=0)
    cur = cand
    cnt = jnp.zeros((1, LANES), F32)
    v16 = jnp.zeros((1, LANES), F32)
    v17 = jnp.zeros((1, LANES), F32)
    for _ in range(N_TOP):
        m = jnp.max(cur, axis=0, keepdims=True)
        eq = cur == m
        new_cnt = cnt + jnp.sum(jnp.where(eq, 1.0, 0.0), axis=0, keepdims=True)
        v16 = jnp.where(cnt < PEER_TOPK, jnp.where(new_cnt >= PEER_TOPK, m, v16), v16)
        v17 = jnp.where(cnt < N_TOP, jnp.where(new_cnt >= N_TOP, m, v17), v17)
        cnt = new_cnt
        cur = jnp.where(eq, -jnp.inf, cur)
    return cand, v16, v17


def _peer_kernel(xn_ref, x_ref, wq_ref, keys_ref, u_ref, vt_ref, out_ref,
                 tau_sc, e0_sc, s1_sc, e1_sc, s0_sc, a_sc, b_sc, acc_sc, s_sc, w_sc, *, n_ph):
    et = pl.program_id(1)
    tb = xn_ref.shape[0]
    nk = keys_ref.shape[1]
    phalf = keys_ref.shape[2]
    e_tile = u_ref.shape[0]
    n_rows = e_tile // nk
    n_chunks = tb // LANES

    @pl.when(et == 0)
    def _():
        acc_sc[...] = jnp.zeros_like(acc_sc)
        a_sc[...] = jnp.full(a_sc.shape, -jnp.inf, F32)
        b_sc[...] = jnp.full(b_sc.shape, -jnp.inf, F32)
        xn = xn_ref[...]
        for h in range(n_ph):
            for c, dst in ((0, s0_sc), (1, s1_sc.at[h])):
                hc = 2 * h + c
                q_t = lax.dot_general(wq_ref[hc * phalf:(hc + 1) * phalf, :], xn, NT_DIMS,
                                      preferred_element_type=F32)
                dst[...] = jnp.dot(keys_ref[hc], q_t.astype(BF16), preferred_element_type=F32)

            def chunk(ci, carry):
                off = pl.multiple_of(ci * LANES, LANES)
                _topk_rows(s0_sc, a_sc, off)
                _topk_rows(s1_sc.at[h], b_sc, off)
                a = a_sc[:, pl.ds(off, LANES)]
                b = b_sc[:, pl.ds(off, LANES)]
                cand, v16, v17 = _pair_threshold(a, b)
                tau = 0.5 * (v16 + v17)
                top = a[0:1] + b[0:1]
                z = jnp.sum(jnp.where(cand >= tau, jnp.exp(cand - top), 0.0), axis=0, keepdims=True)
                s0 = s0_sc[:, pl.ds(off, LANES)]
                tau_sc[h, :, pl.ds(off, LANES)] = tau - s0
                e0_sc[h, :, pl.ds(off, LANES)] = jnp.exp(s0 - a[0:1]) / z
                e1_sc[h, :, pl.ds(off, LANES)] = jnp.exp(s1_sc[h, :, pl.ds(off, LANES)] - b[0:1])
                return carry

            lax.fori_loop(0, n_chunks, chunk, 0)

    s_sc[...] = lax.dot_general(u_ref[...], xn_ref[...], NT_DIMS, preferred_element_type=F32)

    def chunk(ci, carry):
        off = pl.multiple_of(ci * LANES, LANES)
        for r in range(n_rows):
            base = pl.multiple_of(et * n_rows + (r // SUBLANES) * SUBLANES, SUBLANES)
            rr = r % SUBLANES
            t0 = [jnp.broadcast_to(tau_sc[h, pl.ds(base, SUBLANES), pl.ds(off, LANES)][rr:rr + 1],
                                   (SUBLANES, LANES)) for h in range(n_ph)]
            ee = [jnp.broadcast_to(e0_sc[h, pl.ds(base, SUBLANES), pl.ds(off, LANES)][rr:rr + 1],
                                   (SUBLANES, LANES)) for h in range(n_ph)]
            for jp in range(nk // (2 * SUBLANES)):
                ws = []
                for jj in range(2):
                    j0 = jp * 2 * SUBLANES + jj * SUBLANES
                    g = None
                    for h in range(n_ph):
                        s1 = s1_sc[h, j0:j0 + SUBLANES, pl.ds(off, LANES)]
                        x1 = e1_sc[h, j0:j0 + SUBLANES, pl.ds(off, LANES)]
                        term = jnp.where(s1 >= t0[h], x1, 0.0) * ee[h]
                        g = term if g is None else g + term
                    sv = s_sc[r * nk + j0:r * nk + j0 + SUBLANES, pl.ds(off, LANES)]
                    ws.append(g * jax.nn.gelu(sv))
                r0 = r * nk + jp * 2 * SUBLANES
                w_sc[r0:r0 + 2 * SUBLANES, pl.ds(off, LANES)] = jnp.concatenate(ws, axis=0).astype(BF16)
        return carry

    lax.fori_loop(0, n_chunks, chunk, 0)
    acc_sc[...] += jnp.dot(vt_ref[...], w_sc[...], preferred_element_type=F32)

    @pl.when(et == pl.num_programs(1) - 1)
    def _():
        out_ref[...] = x_ref[...] + acc_sc[...].T


def _peer(xn2, x2d, wq_t, keys, u_b, v_t):
    t, d = x2d.shape
    n_hc, nk, phalf = keys.shape
    n_ph = n_hc // 2
    n_exp = u_b.shape[0]
    tb = _pick(t, (512, 256, 128))
    e_tile = SUBLANES * nk
    assert n_exp % e_tile == 0 and t % LANES == 0
    kern = functools.partial(_peer_kernel, n_ph=n_ph)
    head_buf = lambda: pltpu.VMEM((n_ph, nk, tb), F32)
    return pl.pallas_call(
        kern,
        grid=(t // tb, n_exp // e_tile),
        in_specs=[pl.BlockSpec((tb, d), lambda i, e: (i, 0)),
                  pl.BlockSpec((tb, d), lambda i, e: (i, 0)),
                  pl.BlockSpec(wq_t.shape, lambda i, e: (0, 0)),
                  pl.BlockSpec(keys.shape, lambda i, e: (0, 0, 0)),
                  pl.BlockSpec((e_tile, d), lambda i, e: (e, 0)),
                  pl.BlockSpec((d, e_tile), lambda i, e: (0, e))],
        out_specs=pl.BlockSpec((tb, d), lambda i, e: (i, 0)),
        out_shape=jax.ShapeDtypeStruct((t, d), F32),
        scratch_shapes=[head_buf(), head_buf(), head_buf(), head_buf(),
                        pltpu.VMEM((nk, tb), F32),
                        pltpu.VMEM((TOP_ROWS, tb), F32), pltpu.VMEM((TOP_ROWS, tb), F32),
                        pltpu.VMEM((d, tb), F32),
                        pltpu.VMEM((e_tile, tb), F32), pltpu.VMEM((e_tile, tb), BF16)],
        compiler_params=_params(("parallel", "arbitrary")),
        name="peer",
    )(xn2, x2d, wq_t, keys, u_b, v_t)


def _norm_kernel(x_ref, w_ref, o_ref):
    o_ref[...] = _rms(x_ref[...], w_ref[...])


def _final_norm(x2d, w):
    t, d = x2d.shape
    tm = _pick(t, (1024, 512, 256, 128))
    return pl.pallas_call(
        _norm_kernel,
        grid=(t // tm,),
        in_specs=[pl.BlockSpec((tm, d), lambda i: (i, 0)), pl.BlockSpec((1, d), lambda i: (0, 0))],
        out_specs=pl.BlockSpec((tm, d), lambda i: (i, 0)),
        out_shape=jax.ShapeDtypeStruct((t, d), F32),
        compiler_params=_params(("parallel",)),
        name="final_norm",
    )(x2d, w.reshape(1, d))


def _prep_layer(l, w, dims):
    m_dim, n_mh, s_dim, cd, n_sh, d = dims
    w_in = w["w_in"][l]
    o = [0]
    def take(n):
        o[0] += n
        return w_in[:, o[0] - n:o[0]]
    qkvo, w_ig, w_fg = take(4 * m_dim), take(n_mh), take(n_mh)
    w_z, w_xbc, w_dt, w_g = take(s_dim), take(cd), take(n_sh), take(2 * d)
    n_small = 2 * n_mh + n_sh
    w_small = jnp.concatenate([w_ig, w_fg, w_dt, jnp.zeros((d, SMALL_W - n_small), F32)], axis=1)
    b_small = jnp.concatenate([w["b_igate"][l], w["b_fgate"][l], w["dt_bias"][l],
                               jnp.zeros((SMALL_W - n_small,), F32)]).reshape(1, SMALL_W)
    keys = w["peer_keys"][l]
    return dict(
        norm1=w["norm1_w"][l].reshape(1, d),
        w_big=jnp.concatenate([qkvo, w_xbc, w_z, w_g], axis=1).astype(BF16),
        w_small=w_small.astype(BF16), b_small=b_small,
        w_a=w["w_a"][l].astype(BF16), w_b=w["w_b"][l].astype(BF16), w_out=w["w_out"][l].astype(BF16),
        norm2=w["norm2_w"][l].reshape(1, d),
        wq_t=w["peer_wq"][l].T.astype(BF16),
        keys=keys.reshape((keys.shape[0] * 2,) + keys.shape[2:]).astype(BF16),
        u_b=w["peer_u"][l].astype(BF16), v_t=w["peer_v"][l].T.astype(BF16),
    )


def _trunk(x, c0, n0, m0, ssm0, conv0, w, prepped, dims):
    m_dim, n_mh, s_dim, cd, n_sh, d = dims
    b, s, _ = x.shape
    x2 = x.reshape(b * s, d)
    outs = [[] for _ in range(5)]
    for l, p in enumerate(prepped):
        big2, small2 = _in_proj(x2, p["norm1"], p["w_big"], p["w_small"], p["b_small"])
        big3 = big2.reshape(b, s, -1)
        small3 = small2.reshape(b, s, SMALL_W)
        small_t = small3.transpose(0, 2, 1)
        hm, c, n, m = _mlstm(big3, small_t, c0[l], n0[l], m0[l], w["mlstm_norm_w"][l])
        ys, ssm, conv = _ssd(big3, small3, small_t, conv0[l], ssm0[l], w["conv_w"][l], w["conv_b"][l],
                             w["a_log"][l], w["d_skip"][l], w["ssm_norm_w"][l], m_dim=m_dim, n_mh=n_mh)
        ga_blk = (4 * m_dim + cd + s_dim) // d
        x1, xn2 = _out_proj(x2, hm.reshape(b * s, -1), ys.reshape(b * s, -1), big2,
                            p["w_a"], p["w_b"], p["w_out"], p["norm2"], ga_blk)
        x2 = _peer(xn2, x1, p["wq_t"], p["keys"], p["u_b"], p["v_t"])
        for lst, val in zip(outs, (c, n.reshape(b, n_mh, -1), m.reshape(b, n_mh), ssm, conv)):
            lst.append(val)
    y = _final_norm(x2, w["final_norm_w"]).reshape(b, s, d)
    return (y,) + tuple(jnp.stack(o) for o in outs)


def kernel(x_prompt, x_sample, state_mlstm_C, state_mlstm_n, state_mlstm_m, state_ssm, cache_conv,
           norm1_w, w_in, b_igate, b_fgate, mlstm_norm_w, conv_w, conv_b, dt_bias, a_log, d_skip,
           ssm_norm_w, w_a, w_b, w_out, norm2_w, peer_wq, peer_keys, peer_u, peer_v, final_norm_w):
    w = dict(norm1_w=norm1_w, w_in=w_in, b_igate=b_igate, b_fgate=b_fgate, mlstm_norm_w=mlstm_norm_w,
             conv_w=conv_w, conv_b=conv_b, dt_bias=dt_bias, a_log=a_log, d_skip=d_skip,
             ssm_norm_w=ssm_norm_w, w_a=w_a, w_b=w_b, w_out=w_out, norm2_w=norm2_w, peer_wq=peer_wq,
             peer_keys=peer_keys, peer_u=peer_u, peer_v=peer_v, final_norm_w=final_norm_w)
    depth, _, n_mh, dh, _ = state_mlstm_C.shape
    n_sh, pdim = state_ssm.shape[2], state_ssm.shape[3]
    d = x_prompt.shape[-1]
    dims = (n_mh * dh, n_mh, n_sh * pdim, cache_conv.shape[-1], n_sh, d)
    prepped = [_prep_layer(l, w, dims) for l in range(depth)]
    bp = x_prompt.shape[0]
    zeros = lambda a: jnp.zeros((depth, bp) + a.shape[2:], F32)
    yp = _trunk(x_prompt, zeros(state_mlstm_C), zeros(state_mlstm_n), zeros(state_mlstm_m),
                zeros(state_ssm), zeros(cache_conv), w, prepped, dims)
    ys = _trunk(x_sample, state_mlstm_C, state_mlstm_n, state_mlstm_m, state_ssm, cache_conv,
                w, prepped, dims)
    return (yp[0], ys[0]) + yp[1:] + ys[1:]
```

```python
import functools

import jax
import jax.numpy as jnp
from jax import lax
from jax.experimental import pallas as pl
from jax.experimental.pallas import tpu as pltpu

F32 = jnp.float32
BF16 = jnp.bfloat16
EPS = 1e-6
CONV_W = 4
PEER_TOPK = 16
LANES = 128
SUBLANES = 8
MXU_COLS = 256
SMALL_W = 128
VMEM_LIMIT = 56 * 1024 * 1024
HIGHEST = lax.Precision.HIGHEST
NT_DIMS = (((1,), (1,)), ((), ()))


def _pick(n, prefs):
    for p in prefs:
        if n % p == 0:
            return p
    return n


def _softplus(x):
    return jnp.maximum(x, 0.0) + jnp.log1p(jnp.exp(-jnp.abs(x)))


def _log_sigmoid(x):
    return jnp.minimum(x, 0.0) - jnp.log1p(jnp.exp(-jnp.abs(x)))


def _rms(x, w):
    return x * lax.rsqrt(jnp.mean(x * x, axis=-1, keepdims=True) + EPS) * w


def _params(sem):
    return pltpu.CompilerParams(dimension_semantics=sem, vmem_limit_bytes=VMEM_LIMIT)


def _in_proj_kernel(x_ref, nw_ref, wbig_ref, wsmall_ref, bsmall_ref, big_ref, small_ref, h_sc):
    @pl.when(pl.program_id(1) == 0)
    def _():
        hb = _rms(x_ref[...], nw_ref[...]).astype(BF16)
        h_sc[...] = hb
        small_ref[...] = jnp.dot(hb, wsmall_ref[...], preferred_element_type=F32) + bsmall_ref[...]

    big_ref[...] = jnp.dot(h_sc[...], wbig_ref[...], preferred_element_type=F32).astype(BF16)


def _in_proj(x2d, nw, w_big, w_small, b_small):
    t, d = x2d.shape
    nbig = w_big.shape[1]
    tm = _pick(t, (1024, 512, 256, 128))
    tn = _pick(nbig, (2048, 1536, 1024, 512, 256, 128))
    return pl.pallas_call(
        _in_proj_kernel,
        grid=(t // tm, nbig // tn),
        in_specs=[
            pl.BlockSpec((tm, d), lambda i, j: (i, 0)),
            pl.BlockSpec((1, d), lambda i, j: (0, 0)),
            pl.BlockSpec((d, tn), lambda i, j: (0, j)),
            pl.BlockSpec((d, SMALL_W), lambda i, j: (0, 0)),
            pl.BlockSpec((1, SMALL_W), lambda i, j: (0, 0)),
        ],
        out_specs=[
            pl.BlockSpec((tm, tn), lambda i, j: (i, j)),
            pl.BlockSpec((tm, SMALL_W), lambda i, j: (i, 0)),
        ],
        out_shape=[jax.ShapeDtypeStruct((t, nbig), BF16), jax.ShapeDtypeStruct((t, SMALL_W), F32)],
        scratch_shapes=[pltpu.VMEM((tm, d), BF16)],
        compiler_params=_params(("parallel", "arbitrary")),
        name="in_proj",
    )(x2d, nw, w_big, w_small, b_small)


def _mlstm_kernel(q_ref, k_ref, v_ref, og_ref, g_ref, c0_ref, n0_ref, m0_ref, nw_ref,
                  h_ref, c_ref, n_ref, m_ref, c_sc, n_sc, m_sc, *, n_heads, scale):
    hh = pl.program_id(1)
    ci = pl.program_id(2)
    lc = q_ref.shape[1]

    @pl.when(ci == 0)
    def _():
        c_sc[...] = c0_ref[0, 0]
        n_sc[...] = n0_ref[0, 0]
        m_sc[...] = m0_ref[0, 0]

    ig = g_ref[0, pl.ds(hh, 1), :]
    lf = _log_sigmoid(g_ref[0, pl.ds(n_heads + hh, 1), :])
    row = lax.broadcasted_iota(jnp.int32, (lc, lc), 0)
    col = lax.broadcasted_iota(jnp.int32, (lc, lc), 1)
    causal = col <= row
    lf8 = jnp.broadcast_to(lf, (SUBLANES, lc))
    fcum_col = lax.dot_general(causal.astype(F32), lf8, NT_DIMS, precision=HIGHEST,
                               preferred_element_type=F32)[:, 0:1]
    fcum_row = jnp.dot(lf8, (row <= col).astype(F32), precision=HIGHEST,
                       preferred_element_type=F32)[0:1, :]

    m_prev = m_sc[...]
    logw = jnp.where(causal, fcum_col - fcum_row + ig, -jnp.inf)
    log_prev = fcum_col + m_prev
    m_t = jnp.maximum(log_prev, jnp.max(logw, axis=1, keepdims=True))
    a_prev = jnp.exp(log_prev - m_t)
    decay = jnp.exp(logw - m_t)

    q = q_ref[0]
    k = k_ref[0]
    v = v_ref[0]
    s = lax.dot_general(q, k, NT_DIMS, preferred_element_type=F32) * scale * decay
    num = (jnp.dot(s.astype(BF16), v, preferred_element_type=F32)
           + a_prev * jnp.dot(q, c_sc[...].astype(BF16), preferred_element_type=F32))
    qn = jnp.sum(q.astype(F32) * n_sc[...], axis=1, keepdims=True)
    den = jnp.sum(s, axis=1, keepdims=True) + a_prev * qn
    hout = num / jnp.maximum(jnp.abs(den), jnp.exp(-m_t))
    y = _rms(hout, nw_ref[0]) * jax.nn.sigmoid(og_ref[0].astype(F32))
    h_ref[0] = y.astype(BF16)

    f_end = fcum_row[:, lc - 1:lc]
    m_new = m_t[lc - 1:lc, :]
    w_row = jnp.exp(f_end - fcum_row + ig - m_new) * scale
    a_end = jnp.exp(f_end + m_prev - m_new)
    kw = (k.astype(F32).T * w_row).astype(BF16)
    c_sc[...] = a_end * c_sc[...] + jnp.dot(kw, v, preferred_element_type=F32)
    w8 = jnp.broadcast_to(w_row, (SUBLANES, lc)).astype(BF16)
    n_sc[...] = a_end * n_sc[...] + jnp.dot(w8, k, preferred_element_type=F32)[0:1, :]
    m_sc[...] = m_new

    @pl.when(ci == pl.num_programs(2) - 1)
    def _():
        c_ref[0, 0] = c_sc[...]
        n_ref[0, 0] = n_sc[...]
        m_ref[0, 0] = m_sc[...]


def _mlstm(big3, small_t, c0, n0, m0, norm_w):
    b, s, _ = big3.shape
    _, nh, dh, _ = c0.shape
    lc = _pick(s, (256, 128, 64, 32, 16))
    kern = functools.partial(_mlstm_kernel, n_heads=nh, scale=float(dh) ** -0.5)
    blk = lambda off: pl.BlockSpec((1, lc, dh), lambda bi, hi, ci: (bi, ci, off * nh + hi))
    st4 = lambda r, c: pl.BlockSpec((1, 1, r, c), lambda bi, hi, ci: (bi, hi, 0, 0))
    return pl.pallas_call(
        kern,
        grid=(b, nh, s // lc),
        in_specs=[blk(0), blk(1), blk(2), blk(3),
                  pl.BlockSpec((1, SUBLANES, lc), lambda bi, hi, ci: (bi, 0, ci)),
                  st4(dh, dh), st4(1, dh), st4(1, 1),
                  pl.BlockSpec((1, 1, dh), lambda bi, hi, ci: (hi, 0, 0))],
        out_specs=[pl.BlockSpec((1, lc, dh), lambda bi, hi, ci: (bi, ci, hi)),
                   st4(dh, dh), st4(1, dh), st4(1, 1)],
        out_shape=[jax.ShapeDtypeStruct((b, s, nh * dh), BF16),
                   jax.ShapeDtypeStruct((b, nh, dh, dh), F32),
                   jax.ShapeDtypeStruct((b, nh, 1, dh), F32),
                   jax.ShapeDtypeStruct((b, nh, 1, 1), F32)],
        scratch_shapes=[pltpu.VMEM((dh, dh), F32), pltpu.VMEM((1, dh), F32), pltpu.VMEM((1, 1), F32)],
        compiler_params=_params(("parallel", "parallel", "arbitrary")),
        name="mlstm",
    )(big3, big3, big3, big3, small_t, c0, n0.reshape(b, nh, 1, dh), m0.reshape(b, nh, 1, 1),
      norm_w.reshape(nh, 1, dh))


def _ssd_kernel(xbc_ref, z_ref, sm_ref, smt_ref, conv0_ref, ssm0_ref, cw_ref, cb_ref,
                alr_ref, alc_ref, dsk_ref, nw_ref, ex_ref,
                ys_ref, ssm_ref, conv_ref, xpad, ht_sc,
                *, dt_off, n_sh, n_groups, n_state, s_dim):
    ci = pl.program_id(1)
    lc = xbc_ref.shape[1]
    gw = s_dim // n_groups
    hpg = n_sh // n_groups
    pdim = gw // hpg
    pad = SUBLANES

    @pl.when(ci == 0)
    def _():
        xpad[pad - (CONV_W - 1):pad, :] = conv0_ref[0]
        for g in range(n_groups):
            ht_sc[g] = ssm0_ref[0, g].T

    xpad[pad:pad + lc, :] = xbc_ref[0].astype(F32)
    conv = cb_ref[...]
    for i in range(CONV_W):
        conv = conv + xpad[pad - (CONV_W - 1) + i:pad - (CONV_W - 1) + i + lc, :] * cw_ref[i:i + 1, :]
    tail = xpad[pad + lc - (CONV_W - 1):pad + lc, :]
    xpad[pad - (CONV_W - 1):pad, :] = tail
    act = conv * jax.nn.sigmoid(conv)
    xs = act[:, :s_dim]
    bm = act[:, s_dim:s_dim + n_groups * n_state]
    cm = act[:, s_dim + n_groups * n_state:]

    dt_col = _softplus(sm_ref[0][:, dt_off:dt_off + n_sh])
    dt_row = _softplus(smt_ref[0][dt_off:dt_off + n_sh, :])
    la_col = dt_col * (-jnp.exp(alr_ref[...]))
    la_row = dt_row * (-jnp.exp(alc_ref[...]))
    row = lax.broadcasted_iota(jnp.int32, (lc, lc), 0)
    col = lax.broadcasted_iota(jnp.int32, (lc, lc), 1)
    causal = col <= row
    cum_col = jnp.dot(causal.astype(F32), la_col, precision=HIGHEST, preferred_element_type=F32)
    cum_row = jnp.dot(la_row, (row <= col).astype(F32), precision=HIGHEST, preferred_element_type=F32)
    w_end = jnp.exp(cum_col[lc - 1:lc, :] - cum_col) * dt_col
    ex = ex_ref[...]
    ecum_x = jnp.dot(jnp.exp(cum_col), ex, precision=HIGHEST, preferred_element_type=F32)
    wend_x = jnp.dot(w_end, ex, precision=HIGHEST, preferred_element_type=F32)
    xw = (xs * wend_x).astype(BF16)
    xs_b = xs.astype(BF16)
    head_of_lane = lax.div(lax.broadcasted_iota(jnp.int32, (1, gw), 1), pdim)

    parts = []
    for g in range(n_groups):
        bm_g = bm[:, g * n_state:(g + 1) * n_state]
        cm_b = cm[:, g * n_state:(g + 1) * n_state].astype(BF16)
        cb = lax.dot_general(cm_b, bm_g.astype(BF16), NT_DIMS, preferred_element_type=F32)
        xg = xs_b[:, g * gw:(g + 1) * gw]
        yg = (jnp.dot(cm_b, ht_sc[g].astype(BF16), preferred_element_type=F32)
              * ecum_x[:, g * gw:(g + 1) * gw])
        for kk in range(hpg):
            hd = g * hpg + kk
            dec = jnp.exp(jnp.where(causal, cum_col[:, hd:hd + 1] - cum_row[hd:hd + 1, :], -jnp.inf))
            wm = (cb * dec * dt_row[hd:hd + 1, :]).astype(BF16)
            xm = jnp.where(head_of_lane == kk, xg, jnp.zeros_like(xg))
            yg = yg + jnp.dot(wm, xm, preferred_element_type=F32)
        ht_sc[g] = (ecum_x[lc - 1:lc, g * gw:(g + 1) * gw] * ht_sc[g]
                    + jnp.dot(bm_g.T.astype(BF16), xw[:, g * gw:(g + 1) * gw], preferred_element_type=F32))
        parts.append(yg)
    y = jnp.concatenate(parts, axis=1) + dsk_ref[...] * xs
    zf = z_ref[0].astype(F32)
    y = y * (zf * jax.nn.sigmoid(zf))
    ys_ref[0] = _rms(y, nw_ref[...]).astype(BF16)

    @pl.when(ci == pl.num_programs(1) - 1)
    def _():
        conv_ref[0] = tail
        for g in range(n_groups):
            ssm_ref[0, g] = ht_sc[g].T


def _ssd(big3, small3, small_t, conv0, ssm0, conv_w, conv_b, a_log, d_skip, norm_w, *, m_dim, n_mh):
    b, s, _ = big3.shape
    _, n_sh, pdim, n_state = ssm0.shape
    cd = conv0.shape[2]
    s_dim = n_sh * pdim
    n_groups = (cd - s_dim) // (2 * n_state)
    gw = s_dim // n_groups
    lc = _pick(s, (256, 128, 64, 32, 16))
    xbc_blk = (4 * m_dim) // cd
    z_blk = (4 * m_dim + cd) // s_dim
    assert xbc_blk * cd == 4 * m_dim and z_blk * s_dim == 4 * m_dim + cd
    kern = functools.partial(_ssd_kernel, dt_off=2 * n_mh, n_sh=n_sh, n_groups=n_groups,
                             n_state=n_state, s_dim=s_dim)
    full2 = lambda r, c: pl.BlockSpec((r, c), lambda bi, ci: (0, 0))
    expander = jnp.repeat(jnp.eye(n_sh, dtype=F32), pdim, axis=1)
    ys, ssm, conv = pl.pallas_call(
        kern,
        grid=(b, s // lc),
        in_specs=[pl.BlockSpec((1, lc, cd), lambda bi, ci: (bi, ci, xbc_blk)),
                  pl.BlockSpec((1, lc, s_dim), lambda bi, ci: (bi, ci, z_blk)),
                  pl.BlockSpec((1, lc, SMALL_W), lambda bi, ci: (bi, ci, 0)),
                  pl.BlockSpec((1, SMALL_W, lc), lambda bi, ci: (bi, 0, ci)),
                  pl.BlockSpec((1, CONV_W - 1, cd), lambda bi, ci: (bi, 0, 0)),
                  pl.BlockSpec((1, n_groups, gw, n_state), lambda bi, ci: (bi, 0, 0, 0)),
                  full2(CONV_W, cd), full2(1, cd), full2(1, n_sh), full2(n_sh, 1),
                  full2(1, s_dim), full2(1, s_dim), full2(n_sh, s_dim)],
        out_specs=[pl.BlockSpec((1, lc, s_dim), lambda bi, ci: (bi, ci, 0)),
                   pl.BlockSpec((1, n_groups, gw, n_state), lambda bi, ci: (bi, 0, 0, 0)),
                   pl.BlockSpec((1, CONV_W - 1, cd), lambda bi, ci: (bi, 0, 0))],
        out_shape=[jax.ShapeDtypeStruct((b, s, s_dim), BF16),
                   jax.ShapeDtypeStruct((b, n_groups, gw, n_state), F32),
                   jax.ShapeDtypeStruct((b, CONV_W - 1, cd), F32)],
        scratch_shapes=[pltpu.VMEM((lc + SUBLANES, cd), F32), pltpu.VMEM((n_groups, n_state, gw), F32)],
        compiler_params=_params(("parallel", "arbitrary")),
        name="ssd",
    )(big3, big3, small3, small_t, conv0, ssm0.reshape(b, n_groups, gw, n_state), conv_w,
      conv_b.reshape(1, cd), a_log.reshape(1, n_sh), a_log.reshape(n_sh, 1),
      jnp.repeat(d_skip, pdim).reshape(1, s_dim), norm_w.reshape(1, s_dim), expander)
    return ys, ssm.reshape(b, n_sh, pdim, n_state), conv


def _out_kernel(x_ref, hm_ref, ys_ref, ga_ref, gb_ref, wa_ref, wb_ref, wo_ref, n2_ref, xo_ref, xn_ref):
    a = jnp.dot(hm_ref[...], wa_ref[...], preferred_element_type=F32)
    bb = jnp.dot(ys_ref[...], wb_ref[...], preferred_element_type=F32)
    mix = jax.nn.sigmoid(ga_ref[...].astype(F32)) * a + jax.nn.sigmoid(gb_ref[...].astype(F32)) * bb
    xo = x_ref[...] + jnp.dot(mix.astype(BF16), wo_ref[...], preferred_element_type=F32)
    xo_ref[...] = xo
    xn_ref[...] = _rms(xo, n2_ref[...]).astype(BF16)


def _out_proj(x2d, hm2, ys2, big2, w_a, w_b, w_out, norm2_w, ga_blk):
    t, d = x2d.shape
    tm = _pick(t, (512, 256, 128))
    full = lambda a: pl.BlockSpec(a.shape, lambda i: (0, 0))
    rows = lambda w: pl.BlockSpec((tm, w), lambda i: (i, 0))
    return pl.pallas_call(
        _out_kernel,
        grid=(t // tm,),
        in_specs=[rows(d), rows(hm2.shape[1]), rows(ys2.shape[1]),
                  pl.BlockSpec((tm, d), lambda i: (i, ga_blk)),
                  pl.BlockSpec((tm, d), lambda i: (i, ga_blk + 1)),
                  full(w_a), full(w_b), full(w_out), full(norm2_w)],
        out_specs=[rows(d), rows(d)],
        out_shape=[jax.ShapeDtypeStruct((t, d), F32), jax.ShapeDtypeStruct((t, d), BF16)],
        compiler_params=_params(("parallel",)),
        name="out_proj",
    )(x2d, hm2, ys2, big2, big2, w_a, w_b, w_out, norm2_w)


N_TOP = PEER_TOPK + 1


def _sort_network(n):
    def merge(lo, hi, r):
        step = 2 * r
        if step < hi - lo:
            yield from merge(lo, hi, step)
            yield from merge(lo + r, hi, step)
            yield from ((i, i + r) for i in range(lo + r, hi - r, step))
        else:
            yield (lo, lo + r)

    def sort(lo, hi):
        if hi > lo:
            mid = lo + (hi - lo) // 2
            yield from sort(lo, mid)
            yield from sort(mid + 1, hi)
            yield from merge(lo, hi, 1)

    return tuple(sort(0, n - 1))


def _all_sublanes(op, x):
    shift = SUBLANES // 2
    while shift:
        x = op(x, pltpu.roll(x, shift, 0))
        shift //= 2
    return x


def _pop_largest(stack, singles, n_pop):
    sub = lax.broadcasted_iota(jnp.int32, (SUBLANES, LANES), 0)
    stack, singles = list(stack), list(singles)
    n_pos = SUBLANES * (1 + len(singles))
    out = []
    for r in range(n_pop):
        heads = [stack[0]] + singles
        m = functools.reduce(jnp.maximum, heads)
        m = _all_sublanes(jnp.maximum, m)
        out.append(m)
        remaining = n_pop - 1 - r
        if not remaining:
            break
        first = functools.reduce(jnp.minimum, [jnp.where(hd == m, sub + SUBLANES * v, n_pos)
                                               for v, hd in enumerate(heads)])
        first = _all_sublanes(jnp.minimum, first)
        hit = sub == first
        for k in range(min(remaining, len(stack))):
            below = stack[k + 1] if k + 1 < len(stack) else jnp.full_like(stack[k], -jnp.inf)
            stack[k] = jnp.where(hit, below, stack[k])
        singles = [jnp.where(sub + SUBLANES * (v + 1) == first, -jnp.inf, t)
                   for v, t in enumerate(singles)]
    return out


def _top_values(tiles):
    v = list(tiles)
    for i, j in _sort_network(len(v)):
        v[i], v[j] = jnp.maximum(v[i], v[j]), jnp.minimum(v[i], v[j])
    return _pop_largest(v, [], N_TOP)


def _top_pair_sums(a, b):
    sub = lax.broadcasted_iota(jnp.int32, (SUBLANES, LANES), 0)
    neg = jnp.full((SUBLANES, LANES), -jnp.inf, F32)

    def b_tile(lo):
        t = neg
        for s in reversed(range(SUBLANES)):
            if lo + s < N_TOP:
                t = jnp.where(sub == s, b[lo + s], t)
        return t

    b0 = b_tile(0)
    stack = [b0 + a[k] for k in range(N_TOP)]
    singles = [b_tile(lo) + a[0] for lo in range(SUBLANES, N_TOP, SUBLANES)]
    return _pop_largest(stack, singles, N_TOP)


def _peer_kernel(xn_ref, x_ref, wq_ref, keys_ref, u_ref, unext_ref, vt_ref, out_ref,
                 tau_sc, e0_sc, s1_sc, e1_sc, s0_sc, acc_sc,
                 sa_sc, sb_sc, wa_sc, wb_sc, vprev_sc, trow_sc, erow_sc, *, n_ph):
    et = pl.program_id(1)
    tb = xn_ref.shape[0]
    nk = keys_ref.shape[1]
    phalf = keys_ref.shape[2]
    e_tile = u_ref.shape[0]

    @pl.when(et == 0)
    def _():
        acc_sc[...] = jnp.zeros_like(acc_sc)
        wb_sc[...] = jnp.zeros_like(wb_sc)
        vprev_sc[...] = jnp.zeros_like(vprev_sc)
        xn = xn_ref[...]

        def head(h, carry):
            for c, dst in ((0, s0_sc), (1, s1_sc)):
                hc = 2 * h + c
                wq_rows = wq_ref[pl.ds(pl.multiple_of(hc * phalf, phalf), phalf), :]
                q_t = lax.dot_general(wq_rows, xn, NT_DIMS, preferred_element_type=F32)
                dst[...] = jnp.dot(keys_ref[hc], q_t.astype(BF16), preferred_element_type=F32)
            for lc in range(tb // LANES):
                lanes = slice(lc * LANES, (lc + 1) * LANES)
                rows = [slice(k * SUBLANES, (k + 1) * SUBLANES) for k in range(nk // SUBLANES)]
                s0 = [s0_sc[r, lanes] for r in rows]
                s1 = [s1_sc[r, lanes] for r in rows]
                a = _top_values(s0)
                b = _top_values(s1)
                top = _top_pair_sums(a, b)
                tau = 0.5 * (top[PEER_TOPK - 1] + top[PEER_TOPK])
                z = functools.reduce(jnp.add, [jnp.exp(t - top[0]) for t in top[:PEER_TOPK]])
                rz = 1.0 / z
                tb1 = tau - b[0]
                for r, t0, t1 in zip(rows, s0, s1):
                    tau_sc[h, r, lanes] = jnp.exp(tb1 - t0)
                    e0_sc[h, r, lanes] = jnp.exp(t0 - a[0]) * rz
                    e1_sc[h, r, lanes] = jnp.exp(t1 - b[0])
            return carry

        lax.fori_loop(0, n_ph, head, 0)
        sa_sc[...] = lax.dot_general(u_ref[0:e_tile // 2, :], xn, NT_DIMS, preferred_element_type=F32)

    half = e_tile // 2

    base = pl.multiple_of(et * 2 * SUBLANES, 2 * SUBLANES)
    for h in range(n_ph):
        trow_sc[h] = tau_sc[h, pl.ds(base, 2 * SUBLANES), :]
        erow_sc[h] = e0_sc[h, pl.ds(base, 2 * SUBLANES), :]

    def gates(tile, s_in, w_out, lc):
        rows = slice(tile * SUBLANES, (tile + 1) * SUBLANES)
        lanes = slice(lc * LANES, (lc + 1) * LANES)
        t_tiles = [trow_sc[h, rows, lanes] for h in range(n_ph)]
        e_tiles = [erow_sc[h, rows, lanes] for h in range(n_ph)]
        for r in range(SUBLANES):
            t0 = [jnp.broadcast_to(t[r:r + 1], (SUBLANES, LANES)) for t in t_tiles]
            ee = [jnp.broadcast_to(e[r:r + 1], (SUBLANES, LANES)) for e in e_tiles]
            for jp in range(nk // (2 * SUBLANES)):
                ws = []
                for jj in range(2):
                    j0 = jp * 2 * SUBLANES + jj * SUBLANES
                    g = None
                    for h in range(n_ph):
                        x1 = e1_sc[h, j0:j0 + SUBLANES, lanes]
                        term = jnp.where(x1 >= t0[h], x1, 0.0) * ee[h]
                        g = term if g is None else g + term
                    ws.append(g * jax.nn.gelu(s_in[r * nk + j0:r * nk + j0 + SUBLANES, lanes]))
                r0 = r * nk + jp * 2 * SUBLANES
                w_out[r0:r0 + 2 * SUBLANES, lanes] = jnp.concatenate(ws, axis=0).astype(BF16)

    d_half = acc_sc.shape[0] // 2
    q_rows = half // 2

    def acc_piece(v_ref, w_ref, p):
        rows = slice(p * d_half, (p + 1) * d_half)
        acc_sc[rows, :] += jnp.dot(v_ref[rows, :], w_ref[...], preferred_element_type=F32)

    def score_piece(s_ref, rows_ref, p):
        rows = slice(p * q_rows, (p + 1) * q_rows)
        s_ref[rows, :] = lax.dot_general(rows_ref[rows, :], xn_ref[...], NT_DIMS,
                                         preferred_element_type=F32)

    ub_ref = u_ref.at[half:, :]
    va_ref = vt_ref.at[:, 0:half]
    mxu_work = [lambda: acc_piece(vprev_sc, wb_sc, 0), lambda: acc_piece(vprev_sc, wb_sc, 1),
                lambda: score_piece(sb_sc, ub_ref, 0), lambda: score_piece(sb_sc, ub_ref, 1),
                lambda: acc_piece(va_ref, wa_sc, 0), lambda: acc_piece(va_ref, wa_sc, 1),
                lambda: score_piece(sa_sc, unext_ref, 0), lambda: score_piece(sa_sc, unext_ref, 1)]
    n_lc = tb // LANES
    per_tile = len(mxu_work) // 2
    for sec in range(2 * n_lc):
        tile, lc = divmod(sec, n_lc)
        gates(tile, (sa_sc, sb_sc)[tile], (wa_sc, wb_sc)[tile], lc)
        for piece in range(lc * per_tile // n_lc, (lc + 1) * per_tile // n_lc):
            mxu_work[tile * per_tile + piece]()
    vprev_sc[...] = vt_ref[:, half:]

    @pl.when(et == pl.num_programs(1) - 1)
    def _():
        tail = jnp.dot(vprev_sc[...], wb_sc[...], preferred_element_type=F32)
        out_ref[...] = x_ref[...] + (acc_sc[...] + tail).T


def _peer(xn2, x2d, wq_t, keys, u_b, v_t):
    t, d = x2d.shape
    n_hc, nk, phalf = keys.shape
    n_ph = n_hc // 2
    n_exp = u_b.shape[0]
    tb = _pick(t, (512, 256))
    half = SUBLANES * nk
    e_tile = 2 * half
    assert n_exp % e_tile == 0 and tb % MXU_COLS == 0
    n_steps = n_exp // e_tile
    kern = functools.partial(_peer_kernel, n_ph=n_ph)
    head_buf = lambda: pltpu.VMEM((n_ph, nk, tb), F32)
    once = dict(pipeline_mode=pl.Buffered(1))
    return pl.pallas_call(
        kern,
        grid=(t // tb, n_exp // e_tile),
        in_specs=[pl.BlockSpec((tb, d), lambda i, e: (i, 0), **once),
                  pl.BlockSpec((tb, d), lambda i, e: (i, 0), **once),
                  pl.BlockSpec(wq_t.shape, lambda i, e: (0, 0), **once),
                  pl.BlockSpec(keys.shape, lambda i, e: (0, 0, 0), **once),
                  pl.BlockSpec((e_tile, d), lambda i, e: (e, 0)),
                  pl.BlockSpec((half, d), lambda i, e: (jnp.minimum(2 * e + 2, 2 * n_steps - 2), 0)),
                  pl.BlockSpec((d, e_tile), lambda i, e: (0, e))],
        out_specs=pl.BlockSpec((tb, d), lambda i, e: (i, 0)),
        out_shape=jax.ShapeDtypeStruct((t, d), F32),
        scratch_shapes=[head_buf(), head_buf(), pltpu.VMEM((nk, tb), F32), head_buf(),
                        pltpu.VMEM((nk, tb), F32),
                        pltpu.VMEM((d, tb), F32),
                        pltpu.VMEM((half, tb), F32), pltpu.VMEM((half, tb), F32),
                        pltpu.VMEM((half, tb), BF16), pltpu.VMEM((half, tb), BF16),
                        pltpu.VMEM((d, half), BF16),
                        pltpu.VMEM((n_ph, 2 * SUBLANES, tb), F32), pltpu.VMEM((n_ph, 2 * SUBLANES, tb), F32)],
        compiler_params=_params(("parallel", "arbitrary")),
        name="peer",
    )(xn2, x2d, wq_t, keys, u_b, u_b, v_t)


def _norm_kernel(x_ref, w_ref, o_ref):
    o_ref[...] = _rms(x_ref[...], w_ref[...])


def _final_norm(x2d, w):
    t, d = x2d.shape
    tm = _pick(t, (1024, 512, 256, 128))
    return pl.pallas_call(
        _norm_kernel,
        grid=(t // tm,),
        in_specs=[pl.BlockSpec((tm, d), lambda i: (i, 0)), pl.BlockSpec((1, d), lambda i: (0, 0))],
        out_specs=pl.BlockSpec((tm, d), lambda i: (i, 0)),
        out_shape=jax.ShapeDtypeStruct((t, d), F32),
        compiler_params=_params(("parallel",)),
        name="final_norm",
    )(x2d, w.reshape(1, d))


def _prep_layer(l, w, dims):
    m_dim, n_mh, s_dim, cd, n_sh, d = dims
    w_in = w["w_in"][l]
    o = [0]
    def take(n):
        o[0] += n
        return w_in[:, o[0] - n:o[0]]
    qkvo, w_ig, w_fg = take(4 * m_dim), take(n_mh), take(n_mh)
    w_z, w_xbc, w_dt, w_g = take(s_dim), take(cd), take(n_sh), take(2 * d)
    n_small = 2 * n_mh + n_sh
    w_small = jnp.concatenate([w_ig, w_fg, w_dt, jnp.zeros((d, SMALL_W - n_small), F32)], axis=1)
    b_small = jnp.concatenate([w["b_igate"][l], w["b_fgate"][l], w["dt_bias"][l],
                               jnp.zeros((SMALL_W - n_small,), F32)]).reshape(1, SMALL_W)
    keys = w["peer_keys"][l]
    return dict(
        norm1=w["norm1_w"][l].reshape(1, d),
        w_big=jnp.concatenate([qkvo, w_xbc, w_z, w_g], axis=1).astype(BF16),
        w_small=w_small.astype(BF16), b_small=b_small,
        w_a=w["w_a"][l].astype(BF16), w_b=w["w_b"][l].astype(BF16), w_out=w["w_out"][l].astype(BF16),
        norm2=w["norm2_w"][l].reshape(1, d),
        wq_t=w["peer_wq"][l].T.astype(BF16),
        keys=keys.reshape((keys.shape[0] * 2,) + keys.shape[2:]).astype(BF16),
        u_b=w["peer_u"][l].astype(BF16), v_t=w["peer_v"][l].T.astype(BF16),
    )


def _trunk(x, c0, n0, m0, ssm0, conv0, w, prepped, dims):
    m_dim, n_mh, s_dim, cd, n_sh, d = dims
    b, s, _ = x.shape
    x2 = x.reshape(b * s, d)
    outs = [[] for _ in range(5)]
    for l, p in enumerate(prepped):
        big2, small2 = _in_proj(x2, p["norm1"], p["w_big"], p["w_small"], p["b_small"])
        big3 = big2.reshape(b, s, -1)
        small3 = small2.reshape(b, s, SMALL_W)
        small_t = small3.transpose(0, 2, 1)
        hm, c, n, m = _mlstm(big3, small_t, c0[l], n0[l], m0[l], w["mlstm_norm_w"][l])
        ys, ssm, conv = _ssd(big3, small3, small_t, conv0[l], ssm0[l], w["conv_w"][l], w["conv_b"][l],
                             w["a_log"][l], w["d_skip"][l], w["ssm_norm_w"][l], m_dim=m_dim, n_mh=n_mh)
        ga_blk = (4 * m_dim + cd + s_dim) // d
        x1, xn2 = _out_proj(x2, hm.reshape(b * s, -1), ys.reshape(b * s, -1), big2,
                            p["w_a"], p["w_b"], p["w_out"], p["norm2"], ga_blk)
        x2 = _peer(xn2, x1, p["wq_t"], p["keys"], p["u_b"], p["v_t"])
        for lst, val in zip(outs, (c, n.reshape(b, n_mh, -1), m.reshape(b, n_mh), ssm, conv)):
            lst.append(val)
    y = _final_norm(x2, w["final_norm_w"]).reshape(b, s, d)
    return (y,) + tuple(jnp.stack(o) for o in outs)


def kernel(x_prompt, x_sample, state_mlstm_C, state_mlstm_n, state_mlstm_m, state_ssm, cache_conv,
           norm1_w, w_in, b_igate, b_fgate, mlstm_norm_w, conv_w, conv_b, dt_bias, a_log, d_skip,
           ssm_norm_w, w_a, w_b, w_out, norm2_w, peer_wq, peer_keys, peer_u, peer_v, final_norm_w):
    w = dict(norm1_w=norm1_w, w_in=w_in, b_igate=b_igate, b_fgate=b_fgate, mlstm_norm_w=mlstm_norm_w,
             conv_w=conv_w, conv_b=conv_b, dt_bias=dt_bias, a_log=a_log, d_skip=d_skip,
             ssm_norm_w=ssm_norm_w, w_a=w_a, w_b=w_b, w_out=w_out, norm2_w=norm2_w, peer_wq=peer_wq,
             peer_keys=peer_keys, peer_u=peer_u, peer_v=peer_v, final_norm_w=final_norm_w)
    depth, _, n_mh, dh, _ = state_mlstm_C.shape
    n_sh, pdim = state_ssm.shape[2], state_ssm.shape[3]
    d = x_prompt.shape[-1]
    dims = (n_mh * dh, n_mh, n_sh * pdim, cache_conv.shape[-1], n_sh, d)
    prepped = [_prep_layer(l, w, dims) for l in range(depth)]
    bp = x_prompt.shape[0]
    zeros = lambda a: jnp.zeros((depth, bp) + a.shape[2:], F32)
    yp = _trunk(x_prompt, zeros(state_mlstm_C), zeros(state_mlstm_n), zeros(state_mlstm_m),
                zeros(state_ssm), zeros(cache_conv), w, prepped, dims)
    ys = _trunk(x_sample, state_mlstm_C, state_mlstm_n, state_mlstm_m, state_ssm, cache_conv,
                w, prepped, dims)
    return (yp[0], ys[0]) + yp[1:] + ys[1:]
```

```python
import functools

import jax
import jax.numpy as jnp
from jax import lax
from jax.experimental import pallas as pl
from jax.experimental.pallas import tpu as pltpu

F32 = jnp.float32
BF16 = jnp.bfloat16
EPS = 1e-6
CONV_W = 4
PEER_TOPK = 16
LANES = 128
SUBLANES = 8
MXU_COLS = 256
SMALL_W = 128
VMEM_LIMIT = 56 * 1024 * 1024
NT_DIMS = (((1,), (1,)), ((), ()))


def _pick(n, prefs):
    for p in prefs:
        if n % p == 0:
            return p
    return n


def _softplus(x):
    return jnp.maximum(x, 0.0) + jnp.log1p(jnp.exp(-jnp.abs(x)))


def _log_sigmoid(x):
    return jnp.minimum(x, 0.0) - jnp.log1p(jnp.exp(-jnp.abs(x)))


def _split3(x):
    hi = x.astype(BF16)
    r1 = x - hi.astype(F32)
    mid = r1.astype(BF16)
    lo = (r1 - mid.astype(F32)).astype(BF16)
    return hi, mid, lo


def _rms(x, w):
    return x * lax.rsqrt(jnp.mean(x * x, axis=-1, keepdims=True) + EPS) * w


def _params(sem):
    return pltpu.CompilerParams(dimension_semantics=sem, vmem_limit_bytes=VMEM_LIMIT)


def _in_proj_kernel(x_ref, nw_ref, wbig_ref, wsmall_ref, bsmall_ref, big_ref, small_ref, h_sc):
    @pl.when(pl.program_id(1) == 0)
    def _():
        hb = _rms(x_ref[...], nw_ref[...]).astype(BF16)
        h_sc[...] = hb
        small_ref[...] = jnp.dot(hb, wsmall_ref[...], preferred_element_type=F32) + bsmall_ref[...]

    big_ref[...] = jnp.dot(h_sc[...], wbig_ref[...], preferred_element_type=F32).astype(BF16)


def _in_proj(x2d, nw, w_big, w_small, b_small):
    t, d = x2d.shape
    nbig = w_big.shape[1]
    tm = _pick(t, (1024, 512, 256, 128))
    tn = _pick(nbig, (2048, 1536, 1024, 512, 256, 128))
    return pl.pallas_call(
        _in_proj_kernel,
        grid=(t // tm, nbig // tn),
        in_specs=[
            pl.BlockSpec((tm, d), lambda i, j: (i, 0)),
            pl.BlockSpec((1, d), lambda i, j: (0, 0)),
            pl.BlockSpec((d, tn), lambda i, j: (0, j)),
            pl.BlockSpec((d, SMALL_W), lambda i, j: (0, 0)),
            pl.BlockSpec((1, SMALL_W), lambda i, j: (0, 0)),
        ],
        out_specs=[
            pl.BlockSpec((tm, tn), lambda i, j: (i, j)),
            pl.BlockSpec((tm, SMALL_W), lambda i, j: (i, 0)),
        ],
        out_shape=[jax.ShapeDtypeStruct((t, nbig), BF16), jax.ShapeDtypeStruct((t, SMALL_W), F32)],
        scratch_shapes=[pltpu.VMEM((tm, d), BF16)],
        compiler_params=_params(("parallel", "arbitrary")),
        name="in_proj",
    )(x2d, nw, w_big, w_small, b_small)


def _mlstm_kernel(q_ref, k_ref, v_ref, og_ref, g_ref, c0_ref, n0_ref, m0_ref, nw_ref,
                  h_ref, c_ref, n_ref, m_ref, c_sc, n_sc, m_sc, *, n_heads, scale):
    ci = pl.program_id(1)
    lc = q_ref.shape[1]
    dh = c_sc.shape[1]

    @pl.when(ci == 0)
    def _():
        c_sc[...] = c0_ref[0]
        n_sc[...] = n0_ref[0]
        m_sc[...] = m0_ref[0]

    row = lax.broadcasted_iota(jnp.int32, (lc, lc), 0)
    col = lax.broadcasted_iota(jnp.int32, (lc, lc), 1)
    causal = col <= row
    tri = causal.astype(BF16)
    triu = (row <= col).astype(BF16)
    gates = g_ref[0]

    for hh in range(n_heads):
        cols = slice(hh * dh, (hh + 1) * dh)
        ig = gates[hh:hh + 1, :]
        lf = _log_sigmoid(gates[n_heads + hh:n_heads + hh + 1, :])
        lf_parts = _split3(jnp.broadcast_to(lf, (SUBLANES, lc)))
        fcum_col = sum(lax.dot_general(tri, p, NT_DIMS, preferred_element_type=F32)
                       for p in lf_parts)[:, 0:1]
        fcum_row = sum(jnp.dot(p, triu, preferred_element_type=F32) for p in lf_parts)[0:1, :]

        m_prev = m_sc[hh]
        logw = jnp.where(causal, fcum_col - fcum_row + ig, -jnp.inf)
        log_prev = fcum_col + m_prev
        m_t = jnp.maximum(log_prev, jnp.max(logw, axis=1, keepdims=True))
        a_prev = jnp.exp(log_prev - m_t)
        decay = jnp.exp(logw - m_t)

        q = q_ref[0, :, cols]
        k = k_ref[0, :, cols]
        v = v_ref[0, :, cols]
        s = lax.dot_general(q, k, NT_DIMS, preferred_element_type=F32) * scale * decay
        num = (jnp.dot(s.astype(BF16), v, preferred_element_type=F32)
               + a_prev * jnp.dot(q, c_sc[hh].astype(BF16), preferred_element_type=F32))
        qn = jnp.sum(q.astype(F32) * n_sc[hh], axis=1, keepdims=True)
        den = jnp.sum(s, axis=1, keepdims=True) + a_prev * qn
        hout = num / jnp.maximum(jnp.abs(den), jnp.exp(-m_t))
        y = _rms(hout, nw_ref[hh]) * jax.nn.sigmoid(og_ref[0, :, cols].astype(F32))
        h_ref[0, :, cols] = y.astype(BF16)

        f_end = fcum_row[:, lc - 1:lc]
        m_new = m_t[lc - 1:lc, :]
        w_row = jnp.exp(f_end - fcum_row + ig - m_new) * scale
        a_end = jnp.exp(f_end + m_prev - m_new)
        kw = (k.astype(F32).T * w_row).astype(BF16)
        c_sc[hh] = a_end * c_sc[hh] + jnp.dot(kw, v, preferred_element_type=F32)
        w8 = jnp.broadcast_to(w_row, (SUBLANES, lc)).astype(BF16)
        n_sc[hh] = a_end * n_sc[hh] + jnp.dot(w8, k, preferred_element_type=F32)[0:1, :]
        m_sc[hh] = m_new

    @pl.when(ci == pl.num_programs(1) - 1)
    def _():
        c_ref[0] = c_sc[...]
        n_ref[0] = n_sc[...]
        m_ref[0] = m_sc[...]


def _mlstm(big3, small_t, c0, n0, m0, norm_w):
    b, s, _ = big3.shape
    _, nh, dh, _ = c0.shape
    m_dim = nh * dh
    lc = _pick(s, (256, 128, 64, 32, 16))
    kern = functools.partial(_mlstm_kernel, n_heads=nh, scale=float(dh) ** -0.5)
    blk = lambda off: pl.BlockSpec((1, lc, m_dim), lambda bi, ci: (bi, ci, off))
    st4 = lambda r, c: pl.BlockSpec((1, nh, r, c), lambda bi, ci: (bi, 0, 0, 0))
    return pl.pallas_call(
        kern,
        grid=(b, s // lc),
        in_specs=[blk(0), blk(1), blk(2), blk(3),
                  pl.BlockSpec((1, SUBLANES, lc), lambda bi, ci: (bi, 0, ci)),
                  st4(dh, dh), st4(1, dh), st4(1, 1),
                  pl.BlockSpec((nh, 1, dh), lambda bi, ci: (0, 0, 0))],
        out_specs=[pl.BlockSpec((1, lc, m_dim), lambda bi, ci: (bi, ci, 0)),
                   st4(dh, dh), st4(1, dh), st4(1, 1)],
        out_shape=[jax.ShapeDtypeStruct((b, s, m_dim), BF16),
                   jax.ShapeDtypeStruct((b, nh, dh, dh), F32),
                   jax.ShapeDtypeStruct((b, nh, 1, dh), F32),
                   jax.ShapeDtypeStruct((b, nh, 1, 1), F32)],
        scratch_shapes=[pltpu.VMEM((nh, dh, dh), F32), pltpu.VMEM((nh, 1, dh), F32),
                        pltpu.VMEM((nh, 1, 1), F32)],
        compiler_params=_params(("parallel", "arbitrary")),
        name="mlstm",
    )(big3, big3, big3, big3, small_t, c0, n0.reshape(b, nh, 1, dh), m0.reshape(b, nh, 1, 1),
      norm_w.reshape(nh, 1, dh))


def _ssd_kernel(xbc_ref, z_ref, sm_ref, smt_ref, conv0_ref, ssm0_ref, cw_ref, cb_ref,
                alr_ref, alc_ref, dsk_ref, nw_ref, ex_ref,
                ys_ref, ssm_ref, conv_ref, xpad, ht_sc,
                *, dt_off, n_sh, n_groups, n_state, s_dim):
    ci = pl.program_id(1)
    lc = xbc_ref.shape[1]
    gw = s_dim // n_groups
    hpg = n_sh // n_groups
    pdim = gw // hpg
    pad = SUBLANES

    @pl.when(ci == 0)
    def _():
        xpad[pad - (CONV_W - 1):pad, :] = conv0_ref[0]
        for g in range(n_groups):
            ht_sc[g] = ssm0_ref[0, g].T

    xpad[pad:pad + lc, :] = xbc_ref[0].astype(F32)
    conv = cb_ref[...]
    for i in range(CONV_W):
        conv = conv + xpad[pad - (CONV_W - 1) + i:pad - (CONV_W - 1) + i + lc, :] * cw_ref[i:i + 1, :]
    tail = xpad[pad + lc - (CONV_W - 1):pad + lc, :]
    xpad[pad - (CONV_W - 1):pad, :] = tail
    act = conv * jax.nn.sigmoid(conv)
    xs = act[:, :s_dim]
    bm = act[:, s_dim:s_dim + n_groups * n_state]
    cm = act[:, s_dim + n_groups * n_state:]

    dt_col = _softplus(sm_ref[0][:, dt_off:dt_off + n_sh])
    dt_row = _softplus(smt_ref[0][dt_off:dt_off + n_sh, :])
    la_col = dt_col * (-jnp.exp(alr_ref[...]))
    la_row = dt_row * (-jnp.exp(alc_ref[...]))
    row = lax.broadcasted_iota(jnp.int32, (lc, lc), 0)
    col = lax.broadcasted_iota(jnp.int32, (lc, lc), 1)
    causal = col <= row
    tri = causal.astype(BF16)
    triu = (row <= col).astype(BF16)
    cum_col = sum(jnp.dot(tri, p, preferred_element_type=F32) for p in _split3(la_col))
    cum_row = sum(jnp.dot(p, triu, preferred_element_type=F32) for p in _split3(la_row))
    w_end = jnp.exp(cum_col[lc - 1:lc, :] - cum_col) * dt_col
    ex3 = ex_ref[...]
    ecum_x = jnp.dot(jnp.concatenate(_split3(jnp.exp(cum_col)), axis=1), ex3, preferred_element_type=F32)
    wend_x = jnp.dot(jnp.concatenate(_split3(w_end), axis=1), ex3, preferred_element_type=F32)
    xw = (xs * wend_x).astype(BF16)
    xs_b = xs.astype(BF16)
    head_of_lane = lax.div(lax.broadcasted_iota(jnp.int32, (1, gw), 1), pdim)

    parts = []
    for g in range(n_groups):
        bm_g = bm[:, g * n_state:(g + 1) * n_state]
        cm_b = cm[:, g * n_state:(g + 1) * n_state].astype(BF16)
        cb = lax.dot_general(cm_b, bm_g.astype(BF16), NT_DIMS, preferred_element_type=F32)
        xg = xs_b[:, g * gw:(g + 1) * gw]
        yg = (jnp.dot(cm_b, ht_sc[g].astype(BF16), preferred_element_type=F32)
              * ecum_x[:, g * gw:(g + 1) * gw])
        for kk in range(hpg):
            hd = g * hpg + kk
            dec = jnp.exp(jnp.where(causal, cum_col[:, hd:hd + 1] - cum_row[hd:hd + 1, :], -jnp.inf))
            wm = (cb * dec * dt_row[hd:hd + 1, :]).astype(BF16)
            xm = jnp.where(head_of_lane == kk, xg, jnp.zeros_like(xg))
            yg = yg + jnp.dot(wm, xm, preferred_element_type=F32)
        ht_sc[g] = (ecum_x[lc - 1:lc, g * gw:(g + 1) * gw] * ht_sc[g]
                    + jnp.dot(bm_g.T.astype(BF16), xw[:, g * gw:(g + 1) * gw], preferred_element_type=F32))
        parts.append(yg)
    y = jnp.concatenate(parts, axis=1) + dsk_ref[...] * xs
    zf = z_ref[0].astype(F32)
    y = y * (zf * jax.nn.sigmoid(zf))
    ys_ref[0] = _rms(y, nw_ref[...]).astype(BF16)

    @pl.when(ci == pl.num_programs(1) - 1)
    def _():
        conv_ref[0] = tail
        for g in range(n_groups):
            ssm_ref[0, g] = ht_sc[g].T


def _ssd(big3, small3, small_t, conv0, ssm0, conv_w, conv_b, a_log, d_skip, norm_w, *, m_dim, n_mh):
    b, s, _ = big3.shape
    _, n_sh, pdim, n_state = ssm0.shape
    cd = conv0.shape[2]
    s_dim = n_sh * pdim
    n_groups = (cd - s_dim) // (2 * n_state)
    gw = s_dim // n_groups
    lc = _pick(s, (256, 128, 64, 32, 16))
    xbc_blk = (4 * m_dim) // cd
    z_blk = (4 * m_dim + cd) // s_dim
    assert xbc_blk * cd == 4 * m_dim and z_blk * s_dim == 4 * m_dim + cd
    kern = functools.partial(_ssd_kernel, dt_off=2 * n_mh, n_sh=n_sh, n_groups=n_groups,
                             n_state=n_state, s_dim=s_dim)
    full2 = lambda r, c: pl.BlockSpec((r, c), lambda bi, ci: (0, 0))
    expander = jnp.tile(jnp.repeat(jnp.eye(n_sh, dtype=BF16), pdim, axis=1), (3, 1))
    ys, ssm, conv = pl.pallas_call(
        kern,
        grid=(b, s // lc),
        in_specs=[pl.BlockSpec((1, lc, cd), lambda bi, ci: (bi, ci, xbc_blk)),
                  pl.BlockSpec((1, lc, s_dim), lambda bi, ci: (bi, ci, z_blk)),
                  pl.BlockSpec((1, lc, SMALL_W), lambda bi, ci: (bi, ci, 0)),
                  pl.BlockSpec((1, SMALL_W, lc), lambda bi, ci: (bi, 0, ci)),
                  pl.BlockSpec((1, CONV_W - 1, cd), lambda bi, ci: (bi, 0, 0)),
                  pl.BlockSpec((1, n_groups, gw, n_state), lambda bi, ci: (bi, 0, 0, 0)),
                  full2(CONV_W, cd), full2(1, cd), full2(1, n_sh), full2(n_sh, 1),
                  full2(1, s_dim), full2(1, s_dim), full2(3 * n_sh, s_dim)],
        out_specs=[pl.BlockSpec((1, lc, s_dim), lambda bi, ci: (bi, ci, 0)),
                   pl.BlockSpec((1, n_groups, gw, n_state), lambda bi, ci: (bi, 0, 0, 0)),
                   pl.BlockSpec((1, CONV_W - 1, cd), lambda bi, ci: (bi, 0, 0))],
        out_shape=[jax.ShapeDtypeStruct((b, s, s_dim), BF16),
                   jax.ShapeDtypeStruct((b, n_groups, gw, n_state), F32),
                   jax.ShapeDtypeStruct((b, CONV_W - 1, cd), F32)],
        scratch_shapes=[pltpu.VMEM((lc + SUBLANES, cd), F32), pltpu.VMEM((n_groups, n_state, gw), F32)],
        compiler_params=_params(("parallel", "arbitrary")),
        name="ssd",
    )(big3, big3, small3, small_t, conv0, ssm0.reshape(b, n_groups, gw, n_state), conv_w,
      conv_b.reshape(1, cd), a_log.reshape(1, n_sh), a_log.reshape(n_sh, 1),
      jnp.repeat(d_skip, pdim).reshape(1, s_dim), norm_w.reshape(1, s_dim), expander)
    return ys, ssm.reshape(b, n_sh, pdim, n_state), conv


def _out_kernel(x_ref, hm_ref, ys_ref, ga_ref, gb_ref, wa_ref, wb_ref, wo_ref, n2_ref, xo_ref, xn_ref):
    a = jnp.dot(hm_ref[...], wa_ref[...], preferred_element_type=F32)
    bb = jnp.dot(ys_ref[...], wb_ref[...], preferred_element_type=F32)
    mix = jax.nn.sigmoid(ga_ref[...].astype(F32)) * a + jax.nn.sigmoid(gb_ref[...].astype(F32)) * bb
    xo = x_ref[...] + jnp.dot(mix.astype(BF16), wo_ref[...], preferred_element_type=F32)
    xo_ref[...] = xo
    xn_ref[...] = _rms(xo, n2_ref[...]).astype(BF16)


def _out_proj(x2d, hm2, ys2, big2, w_a, w_b, w_out, norm2_w, ga_blk):
    t, d = x2d.shape
    tm = _pick(t, (512, 256, 128))
    full = lambda a: pl.BlockSpec(a.shape, lambda i: (0, 0))
    rows = lambda w: pl.BlockSpec((tm, w), lambda i: (i, 0))
    return pl.pallas_call(
        _out_kernel,
        grid=(t // tm,),
        in_specs=[rows(d), rows(hm2.shape[1]), rows(ys2.shape[1]),
                  pl.BlockSpec((tm, d), lambda i: (i, ga_blk)),
                  pl.BlockSpec((tm, d), lambda i: (i, ga_blk + 1)),
                  full(w_a), full(w_b), full(w_out), full(norm2_w)],
        out_specs=[rows(d), rows(d)],
        out_shape=[jax.ShapeDtypeStruct((t, d), F32), jax.ShapeDtypeStruct((t, d), BF16)],
        compiler_params=_params(("parallel",)),
        name="out_proj",
    )(x2d, hm2, ys2, big2, big2, w_a, w_b, w_out, norm2_w)


N_TOP = PEER_TOPK + 1
GELU_C1 = (2.0 / 3.141592653589793) ** 0.5
GELU_C2 = GELU_C1 * 0.044715


def _sort_network(n):
    def merge(lo, hi, r):
        step = 2 * r
        if step < hi - lo:
            yield from merge(lo, hi, step)
            yield from merge(lo + r, hi, step)
            yield from ((i, i + r) for i in range(lo + r, hi - r, step))
        else:
            yield (lo, lo + r)

    def sort(lo, hi):
        if hi > lo:
            mid = lo + (hi - lo) // 2
            yield from sort(lo, mid)
            yield from sort(mid + 1, hi)
            yield from merge(lo, hi, 1)

    return tuple(sort(0, n - 1))


def _all_sublanes(op, x):
    shift = SUBLANES // 2
    while shift:
        x = op(x, pltpu.roll(x, shift, 0))
        shift //= 2
    return x


def _pop_largest(stack, singles, n_pop):
    sub = lax.broadcasted_iota(jnp.int32, (SUBLANES, LANES), 0)
    stack, singles = list(stack), list(singles)
    n_pos = SUBLANES * (1 + len(singles))
    out = []
    for r in range(n_pop):
        heads = [stack[0]] + singles
        m = functools.reduce(jnp.maximum, heads)
        m = _all_sublanes(jnp.maximum, m)
        out.append(m)
        remaining = n_pop - 1 - r
        if not remaining:
            break
        first = functools.reduce(jnp.minimum, [jnp.where(hd == m, sub + SUBLANES * v, n_pos)
                                               for v, hd in enumerate(heads)])
        first = _all_sublanes(jnp.minimum, first)
        hit = sub == first
        for k in range(min(remaining, len(stack))):
            below = stack[k + 1] if k + 1 < len(stack) else jnp.full_like(stack[k], -jnp.inf)
            stack[k] = jnp.where(hit, below, stack[k])
        singles = [jnp.where(sub + SUBLANES * (v + 1) == first, -jnp.inf, t)
                   for v, t in enumerate(singles)]
    return out


def _top_values(tiles):
    v = list(tiles)
    for i, j in _sort_network(len(v)):
        v[i], v[j] = jnp.maximum(v[i], v[j]), jnp.minimum(v[i], v[j])
    return _pop_largest(v, [], N_TOP)


def _top_pair_sums(a, b):
    sub = lax.broadcasted_iota(jnp.int32, (SUBLANES, LANES), 0)
    neg = jnp.full((SUBLANES, LANES), -jnp.inf, F32)

    def b_tile(lo):
        t = neg
        for s in reversed(range(SUBLANES)):
            if lo + s < N_TOP:
                t = jnp.where(sub == s, b[lo + s], t)
        return t

    b0 = b_tile(0)
    stack = [b0 + a[k] for k in range(N_TOP)]
    singles = [b_tile(lo) + a[0] for lo in range(SUBLANES, N_TOP, SUBLANES)]
    return _pop_largest(stack, singles, N_TOP)


def _peer_kernel(xn_ref, x_ref, wq_ref, keys_ref, u_ref, unext_ref, vt_ref, out_ref,
                 tau_sc, e0_sc, s1_sc, e1_sc, s0_sc, acc_sc,
                 sa_sc, sb_sc, wa_sc, wb_sc, vprev_sc, trow_sc, erow_sc, *, n_ph):
    et = pl.program_id(1)
    tb = xn_ref.shape[0]
    nk = keys_ref.shape[1]
    phalf = keys_ref.shape[2]
    e_tile = u_ref.shape[0]

    @pl.when(et == 0)
    def _():
        acc_sc[...] = jnp.zeros_like(acc_sc)
        wb_sc[...] = jnp.zeros_like(wb_sc)
        vprev_sc[...] = jnp.zeros_like(vprev_sc)
        xn = xn_ref[...]

        def head(h, carry):
            for c, dst in ((0, s0_sc), (1, s1_sc)):
                hc = 2 * h + c
                wq_rows = wq_ref[pl.ds(pl.multiple_of(hc * phalf, phalf), phalf), :]
                q_t = lax.dot_general(wq_rows, xn, NT_DIMS, preferred_element_type=F32)
                dst[...] = jnp.dot(keys_ref[hc], q_t.astype(BF16), preferred_element_type=F32)
            for lc in range(tb // LANES):
                lanes = slice(lc * LANES, (lc + 1) * LANES)
                rows = [slice(k * SUBLANES, (k + 1) * SUBLANES) for k in range(nk // SUBLANES)]
                s0 = [s0_sc[r, lanes] for r in rows]
                s1 = [s1_sc[r, lanes] for r in rows]
                a = _top_values(s0)
                b = _top_values(s1)
                top = _top_pair_sums(a, b)
                tau = 0.5 * (top[PEER_TOPK - 1] + top[PEER_TOPK])
                z = functools.reduce(jnp.add, [jnp.exp(t - top[0]) for t in top[:PEER_TOPK]])
                rz = 0.5 / z
                tb1 = tau - b[0]
                for r, t0, t1 in zip(rows, s0, s1):
                    tau_sc[h, r, lanes] = jnp.exp(tb1 - t0)
                    e0_sc[h, r, lanes] = jnp.exp(t0 - a[0]) * rz
                    e1_sc[h, r, lanes] = jnp.exp(t1 - b[0])
            return carry

        lax.fori_loop(0, n_ph, head, 0)
        sa_sc[...] = lax.dot_general(u_ref[0:e_tile // 2, :], xn, NT_DIMS, preferred_element_type=F32)

    half = e_tile // 2

    base = pl.multiple_of(et * 2 * SUBLANES, 2 * SUBLANES)
    for h in range(n_ph):
        trow_sc[h] = tau_sc[h, pl.ds(base, 2 * SUBLANES), :]
        erow_sc[h] = e0_sc[h, pl.ds(base, 2 * SUBLANES), :]

    def gates(tile, s_in, w_out, lc):
        rows = slice(tile * SUBLANES, (tile + 1) * SUBLANES)
        lanes = slice(lc * LANES, (lc + 1) * LANES)
        n_jt = nk // SUBLANES
        for r in range(SUBLANES):
            g = [None] * n_jt
            for h in range(n_ph):
                t0 = jnp.broadcast_to(trow_sc[h, rows, lanes][r:r + 1], (SUBLANES, LANES))
                ee = jnp.broadcast_to(erow_sc[h, rows, lanes][r:r + 1], (SUBLANES, LANES))
                for jt in range(n_jt):
                    x1 = e1_sc[h, jt * SUBLANES:(jt + 1) * SUBLANES, lanes]
                    term = jnp.where(x1 >= t0, x1, 0.0) * ee
                    g[jt] = term if g[jt] is None else g[jt] + term
            for jp in range(n_jt // 2):
                r0 = r * nk + jp * 2 * SUBLANES
                w_pair = []
                for jj in range(2):
                    sv = s_in[r0 + jj * SUBLANES:r0 + (jj + 1) * SUBLANES, lanes]
                    inner = sv * (GELU_C1 + GELU_C2 * (sv * sv))
                    w_pair.append(g[2 * jp + jj] * (sv + sv * jnp.tanh(inner)))
                w_out[r0:r0 + 2 * SUBLANES, lanes] = jnp.concatenate(w_pair, axis=0).astype(BF16)

    d_half = acc_sc.shape[0] // 2
    q_rows = half // 2

    def acc_piece(v_ref, w_ref, p):
        rows = slice(p * d_half, (p + 1) * d_half)
        acc_sc[rows, :] += jnp.dot(v_ref[rows, :], w_ref[...], preferred_element_type=F32)

    def score_piece(s_ref, rows_ref, p):
        rows = slice(p * q_rows, (p + 1) * q_rows)
        s_ref[rows, :] = lax.dot_general(rows_ref[rows, :], xn_ref[...], NT_DIMS,
                                         preferred_element_type=F32)

    ub_ref = u_ref.at[half:, :]
    va_ref = vt_ref.at[:, 0:half]
    mxu_work = [lambda: acc_piece(vprev_sc, wb_sc, 0), lambda: acc_piece(vprev_sc, wb_sc, 1),
                lambda: score_piece(sb_sc, ub_ref, 0), lambda: score_piece(sb_sc, ub_ref, 1),
                lambda: acc_piece(va_ref, wa_sc, 0), lambda: acc_piece(va_ref, wa_sc, 1),
                lambda: score_piece(sa_sc, unext_ref, 0), lambda: score_piece(sa_sc, unext_ref, 1)]
    n_lc = tb // LANES
    per_tile = len(mxu_work) // 2
    for sec in range(2 * n_lc):
        tile, lc = divmod(sec, n_lc)

        gates(tile, (sa_sc, sb_sc)[tile], (wa_sc, wb_sc)[tile], lc)
        for piece in range(lc * per_tile // n_lc, (lc + 1) * per_tile // n_lc):
            mxu_work[tile * per_tile + piece]()
    vprev_sc[...] = vt_ref[:, half:]

    @pl.when(et == pl.num_programs(1) - 1)
    def _():
        tail = jnp.dot(vprev_sc[...], wb_sc[...], preferred_element_type=F32)
        out_ref[...] = x_ref[...] + (acc_sc[...] + tail).T


def _peer(xn2, x2d, wq_t, keys, u_b, v_t):
    t, d = x2d.shape
    n_hc, nk, phalf = keys.shape
    n_ph = n_hc // 2
    n_exp = u_b.shape[0]
    tb = _pick(t, (512, 256))
    half = SUBLANES * nk
    e_tile = 2 * half
    assert n_exp % e_tile == 0 and tb % MXU_COLS == 0
    n_steps = n_exp // e_tile
    kern = functools.partial(_peer_kernel, n_ph=n_ph)
    head_buf = lambda: pltpu.VMEM((n_ph, nk, tb), F32)
    once = dict(pipeline_mode=pl.Buffered(1))
    return pl.pallas_call(
        kern,
        grid=(t // tb, n_exp // e_tile),
        in_specs=[pl.BlockSpec((tb, d), lambda i, e: (i, 0), **once),
                  pl.BlockSpec((tb, d), lambda i, e: (i, 0), **once),
                  pl.BlockSpec(wq_t.shape, lambda i, e: (0, 0), **once),
                  pl.BlockSpec(keys.shape, lambda i, e: (0, 0, 0), **once),
                  pl.BlockSpec((e_tile, d), lambda i, e: (e, 0)),
                  pl.BlockSpec((half, d), lambda i, e: (jnp.minimum(2 * e + 2, 2 * n_steps - 2), 0)),
                  pl.BlockSpec((d, e_tile), lambda i, e: (0, e))],
        out_specs=pl.BlockSpec((tb, d), lambda i, e: (i, 0)),
        out_shape=jax.ShapeDtypeStruct((t, d), F32),
        scratch_shapes=[head_buf(), head_buf(), pltpu.VMEM((nk, tb), F32), head_buf(),
                        pltpu.VMEM((nk, tb), F32),
                        pltpu.VMEM((d, tb), F32),
                        pltpu.VMEM((half, tb), F32), pltpu.VMEM((half, tb), F32),
                        pltpu.VMEM((half, tb), BF16), pltpu.VMEM((half, tb), BF16),
                        pltpu.VMEM((d, half), BF16),
                        pltpu.VMEM((n_ph, 2 * SUBLANES, tb), F32), pltpu.VMEM((n_ph, 2 * SUBLANES, tb), F32)],
        compiler_params=_params(("parallel", "arbitrary")),
        name="peer",
    )(xn2, x2d, wq_t, keys, u_b, u_b, v_t)


def _norm_kernel(x_ref, w_ref, o_ref):
    o_ref[...] = _rms(x_ref[...], w_ref[...])


def _final_norm(x2d, w):
    t, d = x2d.shape
    tm = _pick(t, (1024, 512, 256, 128))
    return pl.pallas_call(
        _norm_kernel,
        grid=(t // tm,),
        in_specs=[pl.BlockSpec((tm, d), lambda i: (i, 0)), pl.BlockSpec((1, d), lambda i: (0, 0))],
        out_specs=pl.BlockSpec((tm, d), lambda i: (i, 0)),
        out_shape=jax.ShapeDtypeStruct((t, d), F32),
        compiler_params=_params(("parallel",)),
        name="final_norm",
    )(x2d, w.reshape(1, d))


def _prep_layer(l, w, dims):
    m_dim, n_mh, s_dim, cd, n_sh, d = dims
    w_in = w["w_in"][l]
    o = [0]
    def take(n):
        o[0] += n
        return w_in[:, o[0] - n:o[0]]
    qkvo, w_ig, w_fg = take(4 * m_dim), take(n_mh), take(n_mh)
    w_z, w_xbc, w_dt, w_g = take(s_dim), take(cd), take(n_sh), take(2 * d)
    n_small = 2 * n_mh + n_sh
    w_small = jnp.concatenate([w_ig, w_fg, w_dt, jnp.zeros((d, SMALL_W - n_small), F32)], axis=1)
    b_small = jnp.concatenate([w["b_igate"][l], w["b_fgate"][l], w["dt_bias"][l],
                               jnp.zeros((SMALL_W - n_small,), F32)]).reshape(1, SMALL_W)
    keys = w["peer_keys"][l]
    return dict(
        norm1=w["norm1_w"][l].reshape(1, d),
        w_big=jnp.concatenate([qkvo, w_xbc, w_z, w_g], axis=1).astype(BF16),
        w_small=w_small.astype(BF16), b_small=b_small,
        w_a=w["w_a"][l].astype(BF16), w_b=w["w_b"][l].astype(BF16), w_out=w["w_out"][l].astype(BF16),
        norm2=w["norm2_w"][l].reshape(1, d),
        wq_t=w["peer_wq"][l].T.astype(BF16),
        keys=keys.reshape((keys.shape[0] * 2,) + keys.shape[2:]).astype(BF16),
        u_b=w["peer_u"][l].astype(BF16), v_t=w["peer_v"][l].T.astype(BF16),
    )


def _trunk(x, c0, n0, m0, ssm0, conv0, w, prepped, dims):
    m_dim, n_mh, s_dim, cd, n_sh, d = dims
    b, s, _ = x.shape
    x2 = x.reshape(b * s, d)
    outs = [[] for _ in range(5)]
    for l, p in enumerate(prepped):
        big2, small2 = _in_proj(x2, p["norm1"], p["w_big"], p["w_small"], p["b_small"])
        big3 = big2.reshape(b, s, -1)
        small3 = small2.reshape(b, s, SMALL_W)
        small_t = small3.transpose(0, 2, 1)
        hm, c, n, m = _mlstm(big3, small_t, c0[l], n0[l], m0[l], w["mlstm_norm_w"][l])
        ys, ssm, conv = _ssd(big3, small3, small_t, conv0[l], ssm0[l], w["conv_w"][l], w["conv_b"][l],
                             w["a_log"][l], w["d_skip"][l], w["ssm_norm_w"][l], m_dim=m_dim, n_mh=n_mh)
        ga_blk = (4 * m_dim + cd + s_dim) // d
        x1, xn2 = _out_proj(x2, hm.reshape(b * s, -1), ys.reshape(b * s, -1), big2,
                            p["w_a"], p["w_b"], p["w_out"], p["norm2"], ga_blk)
        x2 = _peer(xn2, x1, p["wq_t"], p["keys"], p["u_b"], p["v_t"])
        for lst, val in zip(outs, (c, n.reshape(b, n_mh, -1), m.reshape(b, n_mh), ssm, conv)):
            lst.append(val)
    y = _final_norm(x2, w["final_norm_w"]).reshape(b, s, d)
    return (y,) + tuple(jnp.stack(o) for o in outs)


def kernel(x_prompt, x_sample, state_mlstm_C, state_mlstm_n, state_mlstm_m, state_ssm, cache_conv,
           norm1_w, w_in, b_igate, b_fgate, mlstm_norm_w, conv_w, conv_b, dt_bias, a_log, d_skip,
           ssm_norm_w, w_a, w_b, w_out, norm2_w, peer_wq, peer_keys, peer_u, peer_v, final_norm_w):
    w = dict(norm1_w=norm1_w, w_in=w_in, b_igate=b_igate, b_fgate=b_fgate, mlstm_norm_w=mlstm_norm_w,
             conv_w=conv_w, conv_b=conv_b, dt_bias=dt_bias, a_log=a_log, d_skip=d_skip,
             ssm_norm_w=ssm_norm_w, w_a=w_a, w_b=w_b, w_out=w_out, norm2_w=norm2_w, peer_wq=peer_wq,
             peer_keys=peer_keys, peer_u=peer_u, peer_v=peer_v, final_norm_w=final_norm_w)
    depth, _, n_mh, dh, _ = state_mlstm_C.shape
    n_sh, pdim = state_ssm.shape[2], state_ssm.shape[3]
    d = x_prompt.shape[-1]
    dims = (n_mh * dh, n_mh, n_sh * pdim, cache_conv.shape[-1], n_sh, d)
    prepped = [_prep_layer(l, w, dims) for l in range(depth)]
    bp = x_prompt.shape[0]
    zeros = lambda a: jnp.zeros((depth, bp) + a.shape[2:], F32)
    yp = _trunk(x_prompt, zeros(state_mlstm_C), zeros(state_mlstm_n), zeros(state_mlstm_m),
                zeros(state_ssm), zeros(cache_conv), w, prepped, dims)
    ys = _trunk(x_sample, state_mlstm_C, state_mlstm_n, state_mlstm_m, state_ssm, cache_conv,
                w, prepped, dims)
    return (yp[0], ys[0]) + yp[1:] + ys[1:]
```

```python
import functools

import jax
import jax.numpy as jnp
from jax import lax
from jax.experimental import pallas as pl
from jax.experimental.pallas import tpu as pltpu

F32 = jnp.float32
BF16 = jnp.bfloat16
EPS = 1e-6
CONV_W = 4
PEER_TOPK = 16
LANES = 128
SUBLANES = 8
MXU_COLS = 256
SMALL_W = 128
VMEM_LIMIT = 56 * 1024 * 1024
NT_DIMS = (((1,), (1,)), ((), ()))


def _pick(n, prefs):
    for p in prefs:
        if n % p == 0:
            return p
    return n


def _softplus(x):
    return jnp.maximum(x, 0.0) + jnp.log1p(jnp.exp(-jnp.abs(x)))


def _log_sigmoid(x):
    return jnp.minimum(x, 0.0) - jnp.log1p(jnp.exp(-jnp.abs(x)))


def _split3(x):
    hi = x.astype(BF16)
    r1 = x - hi.astype(F32)
    mid = r1.astype(BF16)
    lo = (r1 - mid.astype(F32)).astype(BF16)
    return hi, mid, lo


def _exact_zero(tile):
    bits = pltpu.bitcast(tile, jnp.uint32)
    sixteen = jnp.uint32(16)
    return pltpu.bitcast(lax.shift_right_logical(lax.shift_right_logical(bits, sixteen), sixteen), F32)


def _rms(x, w):
    return x * lax.rsqrt(jnp.mean(x * x, axis=-1, keepdims=True) + EPS) * w


def _params(sem):
    return pltpu.CompilerParams(dimension_semantics=sem, vmem_limit_bytes=VMEM_LIMIT)


def _in_proj_kernel(x_ref, nw_ref, wbig_ref, wsmall_ref, bsmall_ref, big_ref, small_ref, h_sc):
    @pl.when(pl.program_id(1) == 0)
    def _():
        hb = _rms(x_ref[...], nw_ref[...]).astype(BF16)
        h_sc[...] = hb
        small_ref[...] = jnp.dot(hb, wsmall_ref[...], preferred_element_type=F32) + bsmall_ref[...]

    big_ref[...] = jnp.dot(h_sc[...], wbig_ref[...], preferred_element_type=F32).astype(BF16)


def _in_proj(x2d, nw, w_big, w_small, b_small):
    t, d = x2d.shape
    nbig = w_big.shape[1]
    tm = _pick(t, (1024, 512, 256, 128))
    tn = _pick(nbig, (2048, 1536, 1024, 512, 256, 128))
    return pl.pallas_call(
        _in_proj_kernel,
        grid=(t // tm, nbig // tn),
        in_specs=[
            pl.BlockSpec((tm, d), lambda i, j: (i, 0)),
            pl.BlockSpec((1, d), lambda i, j: (0, 0)),
            pl.BlockSpec((d, tn), lambda i, j: (0, j)),
            pl.BlockSpec((d, SMALL_W), lambda i, j: (0, 0)),
            pl.BlockSpec((1, SMALL_W), lambda i, j: (0, 0)),
        ],
        out_specs=[
            pl.BlockSpec((tm, tn), lambda i, j: (i, j)),
            pl.BlockSpec((tm, SMALL_W), lambda i, j: (i, 0)),
        ],
        out_shape=[jax.ShapeDtypeStruct((t, nbig), BF16), jax.ShapeDtypeStruct((t, SMALL_W), F32)],
        scratch_shapes=[pltpu.VMEM((tm, d), BF16)],
        compiler_params=_params(("parallel", "arbitrary")),
        name="in_proj",
    )(x2d, nw, w_big, w_small, b_small)


def _mlstm_kernel(q_ref, k_ref, v_ref, og_ref, g_ref, c0_ref, n0_ref, m0_ref, nw_ref,
                  h_ref, c_ref, n_ref, m_ref, c_sc, n_sc, m_sc, *, n_heads, scale):
    ci = pl.program_id(1)
    lc = q_ref.shape[1]
    dh = c_sc.shape[1]

    @pl.when(ci == 0)
    def _():
        c_sc[...] = c0_ref[0]
        n_sc[...] = n0_ref[0]
        m_sc[...] = m0_ref[0]

    row = lax.broadcasted_iota(jnp.int32, (lc, lc), 0)
    col = lax.broadcasted_iota(jnp.int32, (lc, lc), 1)
    causal = col <= row
    tri = causal.astype(BF16)
    triu = (row <= col).astype(BF16)
    gates = g_ref[0]

    for hh in range(n_heads):
        cols = slice(hh * dh, (hh + 1) * dh)
        ig = gates[hh:hh + 1, :]
        lf = _log_sigmoid(gates[n_heads + hh:n_heads + hh + 1, :])
        lf_parts = _split3(jnp.broadcast_to(lf, (SUBLANES, lc)))
        fcum_col = sum(lax.dot_general(tri, p, NT_DIMS, preferred_element_type=F32)
                       for p in lf_parts)[:, 0:1]
        fcum_row = sum(jnp.dot(p, triu, preferred_element_type=F32) for p in lf_parts)[0:1, :]

        m_prev = m_sc[hh]
        logw = jnp.where(causal, fcum_col - fcum_row + ig, -jnp.inf)
        log_prev = fcum_col + m_prev
        m_t = jnp.maximum(log_prev, jnp.max(logw, axis=1, keepdims=True))
        a_prev = jnp.exp(log_prev - m_t)
        decay = jnp.exp(logw - m_t)

        q = q_ref[0, :, cols]
        k = k_ref[0, :, cols]
        v = v_ref[0, :, cols]
        s = lax.dot_general(q, k, NT_DIMS, preferred_element_type=F32) * scale * decay
        num = (jnp.dot(s.astype(BF16), v, preferred_element_type=F32)
               + a_prev * jnp.dot(q, c_sc[hh].astype(BF16), preferred_element_type=F32))
        qn = jnp.sum(q.astype(F32) * n_sc[hh], axis=1, keepdims=True)
        den = jnp.sum(s, axis=1, keepdims=True) + a_prev * qn
        hout = num / jnp.maximum(jnp.abs(den), jnp.exp(-m_t))
        y = _rms(hout, nw_ref[hh]) * jax.nn.sigmoid(og_ref[0, :, cols].astype(F32))
        h_ref[0, :, cols] = y.astype(BF16)

        f_end = fcum_row[:, lc - 1:lc]
        m_new = m_t[lc - 1:lc, :]
        w_row = jnp.exp(f_end - fcum_row + ig - m_new) * scale
        a_end = jnp.exp(f_end + m_prev - m_new)
        kw = (k.astype(F32).T * w_row).astype(BF16)
        c_sc[hh] = a_end * c_sc[hh] + jnp.dot(kw, v, preferred_element_type=F32)
        w8 = jnp.broadcast_to(w_row, (SUBLANES, lc)).astype(BF16)
        n_sc[hh] = a_end * n_sc[hh] + jnp.dot(w8, k, preferred_element_type=F32)[0:1, :]
        m_sc[hh] = m_new

    @pl.when(ci == pl.num_programs(1) - 1)
    def _():
        c_ref[0] = c_sc[...]
        n_ref[0] = n_sc[...]
        m_ref[0] = m_sc[...]


def _mlstm(big3, small_t, c0, n0, m0, norm_w):
    b, s, _ = big3.shape
    _, nh, dh, _ = c0.shape
    m_dim = nh * dh
    lc = _pick(s, (256, 128, 64, 32, 16))
    kern = functools.partial(_mlstm_kernel, n_heads=nh, scale=float(dh) ** -0.5)
    blk = lambda off: pl.BlockSpec((1, lc, m_dim), lambda bi, ci: (bi, ci, off))
    st4 = lambda r, c: pl.BlockSpec((1, nh, r, c), lambda bi, ci: (bi, 0, 0, 0))
    return pl.pallas_call(
        kern,
        grid=(b, s // lc),
        in_specs=[blk(0), blk(1), blk(2), blk(3),
                  pl.BlockSpec((1, SUBLANES, lc), lambda bi, ci: (bi, 0, ci)),
                  st4(dh, dh), st4(1, dh), st4(1, 1),
                  pl.BlockSpec((nh, 1, dh), lambda bi, ci: (0, 0, 0))],
        out_specs=[pl.BlockSpec((1, lc, m_dim), lambda bi, ci: (bi, ci, 0)),
                   st4(dh, dh), st4(1, dh), st4(1, 1)],
        out_shape=[jax.ShapeDtypeStruct((b, s, m_dim), BF16),
                   jax.ShapeDtypeStruct((b, nh, dh, dh), F32),
                   jax.ShapeDtypeStruct((b, nh, 1, dh), F32),
                   jax.ShapeDtypeStruct((b, nh, 1, 1), F32)],
        scratch_shapes=[pltpu.VMEM((nh, dh, dh), F32), pltpu.VMEM((nh, 1, dh), F32),
                        pltpu.VMEM((nh, 1, 1), F32)],
        compiler_params=_params(("parallel", "arbitrary")),
        name="mlstm",
    )(big3, big3, big3, big3, small_t, c0, n0.reshape(b, nh, 1, dh), m0.reshape(b, nh, 1, 1),
      norm_w.reshape(nh, 1, dh))


def _ssd_kernel(xbc_ref, z_ref, sm_ref, smt_ref, conv0_ref, ssm0_ref, cw_ref, cb_ref,
                alr_ref, alc_ref, dsk_ref, nw_ref, ex_ref,
                ys_ref, ssm_ref, conv_ref, xpad, ht_sc,
                *, dt_off, n_sh, n_groups, n_state, s_dim):
    ci = pl.program_id(1)
    lc = xbc_ref.shape[1]
    gw = s_dim // n_groups
    hpg = n_sh // n_groups
    pdim = gw // hpg
    pad = SUBLANES

    @pl.when(ci == 0)
    def _():
        xpad[pad - (CONV_W - 1):pad, :] = conv0_ref[0]
        for g in range(n_groups):
            ht_sc[g] = ssm0_ref[0, g].T

    xpad[pad:pad + lc, :] = xbc_ref[0].astype(F32)
    conv = cb_ref[...]
    for i in range(CONV_W):
        conv = conv + xpad[pad - (CONV_W - 1) + i:pad - (CONV_W - 1) + i + lc, :] * cw_ref[i:i + 1, :]
    tail = xpad[pad + lc - (CONV_W - 1):pad + lc, :]
    xpad[pad - (CONV_W - 1):pad, :] = tail
    act = conv * jax.nn.sigmoid(conv)
    xs = act[:, :s_dim]
    bm = act[:, s_dim:s_dim + n_groups * n_state]
    cm = act[:, s_dim + n_groups * n_state:]

    dt_col = _softplus(sm_ref[0][:, dt_off:dt_off + n_sh])
    dt_row = _softplus(smt_ref[0][dt_off:dt_off + n_sh, :])
    la_col = dt_col * (-jnp.exp(alr_ref[...]))
    la_row = dt_row * (-jnp.exp(alc_ref[...]))
    row = lax.broadcasted_iota(jnp.int32, (lc, lc), 0)
    col = lax.broadcasted_iota(jnp.int32, (lc, lc), 1)
    causal = col <= row
    tri = causal.astype(BF16)
    triu = (row <= col).astype(BF16)
    cum_col = sum(jnp.dot(tri, p, preferred_element_type=F32) for p in _split3(la_col))
    cum_row = sum(jnp.dot(p, triu, preferred_element_type=F32) for p in _split3(la_row))
    w_end = jnp.exp(cum_col[lc - 1:lc, :] - cum_col) * dt_col
    ex3 = ex_ref[...]
    ecum_x = jnp.dot(jnp.concatenate(_split3(jnp.exp(cum_col)), axis=1), ex3, preferred_element_type=F32)
    wend_x = jnp.dot(jnp.concatenate(_split3(w_end), axis=1), ex3, preferred_element_type=F32)
    xw = (xs * wend_x).astype(BF16)
    xs_b = xs.astype(BF16)
    head_of_lane = lax.div(lax.broadcasted_iota(jnp.int32, (1, gw), 1), pdim)

    parts = []
    for g in range(n_groups):
        bm_g = bm[:, g * n_state:(g + 1) * n_state]
        cm_b = cm[:, g * n_state:(g + 1) * n_state].astype(BF16)
        cb = lax.dot_general(cm_b, bm_g.astype(BF16), NT_DIMS, preferred_element_type=F32)
        xg = xs_b[:, g * gw:(g + 1) * gw]
        yg = (jnp.dot(cm_b, ht_sc[g].astype(BF16), preferred_element_type=F32)
              * ecum_x[:, g * gw:(g + 1) * gw])
        for kk in range(hpg):
            hd = g * hpg + kk
            dec = jnp.exp(jnp.where(causal, cum_col[:, hd:hd + 1] - cum_row[hd:hd + 1, :], -jnp.inf))
            wm = (cb * dec * dt_row[hd:hd + 1, :]).astype(BF16)
            xm = jnp.where(head_of_lane == kk, xg, jnp.zeros_like(xg))
            yg = yg + jnp.dot(wm, xm, preferred_element_type=F32)
        ht_sc[g] = (ecum_x[lc - 1:lc, g * gw:(g + 1) * gw] * ht_sc[g]
                    + jnp.dot(bm_g.T.astype(BF16), xw[:, g * gw:(g + 1) * gw], preferred_element_type=F32))
        parts.append(yg)
    y = jnp.concatenate(parts, axis=1) + dsk_ref[...] * xs
    zf = z_ref[0].astype(F32)
    y = y * (zf * jax.nn.sigmoid(zf))
    ys_ref[0] = _rms(y, nw_ref[...]).astype(BF16)

    @pl.when(ci == pl.num_programs(1) - 1)
    def _():
        conv_ref[0] = tail
        for g in range(n_groups):
            ssm_ref[0, g] = ht_sc[g].T


def _ssd(big3, small3, small_t, conv0, ssm0, conv_w, conv_b, a_log, d_skip, norm_w, *, m_dim, n_mh):
    b, s, _ = big3.shape
    _, n_sh, pdim, n_state = ssm0.shape
    cd = conv0.shape[2]
    s_dim = n_sh * pdim
    n_groups = (cd - s_dim) // (2 * n_state)
    gw = s_dim // n_groups
    lc = _pick(s, (256, 128, 64, 32, 16))
    xbc_blk = (4 * m_dim) // cd
    z_blk = (4 * m_dim + cd) // s_dim
    assert xbc_blk * cd == 4 * m_dim and z_blk * s_dim == 4 * m_dim + cd
    kern = functools.partial(_ssd_kernel, dt_off=2 * n_mh, n_sh=n_sh, n_groups=n_groups,
                             n_state=n_state, s_dim=s_dim)
    full2 = lambda r, c: pl.BlockSpec((r, c), lambda bi, ci: (0, 0))
    expander = jnp.tile(jnp.repeat(jnp.eye(n_sh, dtype=BF16), pdim, axis=1), (3, 1))
    ys, ssm, conv = pl.pallas_call(
        kern,
        grid=(b, s // lc),
        in_specs=[pl.BlockSpec((1, lc, cd), lambda bi, ci: (bi, ci, xbc_blk)),
                  pl.BlockSpec((1, lc, s_dim), lambda bi, ci: (bi, ci, z_blk)),
                  pl.BlockSpec((1, lc, SMALL_W), lambda bi, ci: (bi, ci, 0)),
                  pl.BlockSpec((1, SMALL_W, lc), lambda bi, ci: (bi, 0, ci)),
                  pl.BlockSpec((1, CONV_W - 1, cd), lambda bi, ci: (bi, 0, 0)),
                  pl.BlockSpec((1, n_groups, gw, n_state), lambda bi, ci: (bi, 0, 0, 0)),
                  full2(CONV_W, cd), full2(1, cd), full2(1, n_sh), full2(n_sh, 1),
                  full2(1, s_dim), full2(1, s_dim), full2(3 * n_sh, s_dim)],
        out_specs=[pl.BlockSpec((1, lc, s_dim), lambda bi, ci: (bi, ci, 0)),
                   pl.BlockSpec((1, n_groups, gw, n_state), lambda bi, ci: (bi, 0, 0, 0)),
                   pl.BlockSpec((1, CONV_W - 1, cd), lambda bi, ci: (bi, 0, 0))],
        out_shape=[jax.ShapeDtypeStruct((b, s, s_dim), BF16),
                   jax.ShapeDtypeStruct((b, n_groups, gw, n_state), F32),
                   jax.ShapeDtypeStruct((b, CONV_W - 1, cd), F32)],
        scratch_shapes=[pltpu.VMEM((lc + SUBLANES, cd), F32), pltpu.VMEM((n_groups, n_state, gw), F32)],
        compiler_params=_params(("parallel", "arbitrary")),
        name="ssd",
    )(big3, big3, small3, small_t, conv0, ssm0.reshape(b, n_groups, gw, n_state), conv_w,
      conv_b.reshape(1, cd), a_log.reshape(1, n_sh), a_log.reshape(n_sh, 1),
      jnp.repeat(d_skip, pdim).reshape(1, s_dim), norm_w.reshape(1, s_dim), expander)
    return ys, ssm.reshape(b, n_sh, pdim, n_state), conv


def _out_kernel(x_ref, hm_ref, ys_ref, ga_ref, gb_ref, wa_ref, wb_ref, wo_ref, n2_ref, xo_ref, xn_ref):
    a = jnp.dot(hm_ref[...], wa_ref[...], preferred_element_type=F32)
    bb = jnp.dot(ys_ref[...], wb_ref[...], preferred_element_type=F32)
    mix = jax.nn.sigmoid(ga_ref[...].astype(F32)) * a + jax.nn.sigmoid(gb_ref[...].astype(F32)) * bb
    xo = x_ref[...] + jnp.dot(mix.astype(BF16), wo_ref[...], preferred_element_type=F32)
    xo_ref[...] = xo
    xn_ref[...] = _rms(xo, n2_ref[...]).astype(BF16)


def _out_proj(x2d, hm2, ys2, big2, w_a, w_b, w_out, norm2_w, ga_blk):
    t, d = x2d.shape
    tm = _pick(t, (512, 256, 128))
    full = lambda a: pl.BlockSpec(a.shape, lambda i: (0, 0))
    rows = lambda w: pl.BlockSpec((tm, w), lambda i: (i, 0))
    return pl.pallas_call(
        _out_kernel,
        grid=(t // tm,),
        in_specs=[rows(d), rows(hm2.shape[1]), rows(ys2.shape[1]),
                  pl.BlockSpec((tm, d), lambda i: (i, ga_blk)),
                  pl.BlockSpec((tm, d), lambda i: (i, ga_blk + 1)),
                  full(w_a), full(w_b), full(w_out), full(norm2_w)],
        out_specs=[rows(d), rows(d)],
        out_shape=[jax.ShapeDtypeStruct((t, d), F32), jax.ShapeDtypeStruct((t, d), BF16)],
        compiler_params=_params(("parallel",)),
        name="out_proj",
    )(x2d, hm2, ys2, big2, big2, w_a, w_b, w_out, norm2_w)


N_TOP = PEER_TOPK + 1
GELU_C1 = (2.0 / 3.141592653589793) ** 0.5
GELU_C2 = GELU_C1 * 0.044715


def _sort_network(n):
    def merge(lo, hi, r):
        step = 2 * r
        if step < hi - lo:
            yield from merge(lo, hi, step)
            yield from merge(lo + r, hi, step)
            yield from ((i, i + r) for i in range(lo + r, hi - r, step))
        else:
            yield (lo, lo + r)

    def sort(lo, hi):
        if hi > lo:
            mid = lo + (hi - lo) // 2
            yield from sort(lo, mid)
            yield from sort(mid + 1, hi)
            yield from merge(lo, hi, 1)

    return tuple(sort(0, n - 1))


def _all_sublanes(op, x):
    shift = SUBLANES // 2
    while shift:
        x = op(x, pltpu.roll(x, shift, 0))
        shift //= 2
    return x


def _merge_columns(v):
    n = len(v)
    dropped = None
    shift = SUBLANES // 2
    while shift:
        partner = [pltpu.roll(v[n - 1 - k], shift, 0) for k in range(n)]
        lo = functools.reduce(jnp.maximum, [jnp.minimum(v[k], partner[k]) for k in range(n)])
        dropped = lo if dropped is None else jnp.maximum(dropped, lo)
        v = [jnp.maximum(v[k], partner[k]) for k in range(n)]
        stride = n // 2
        while stride:
            for i in range(n):
                if not i & stride:
                    v[i], v[i + stride] = jnp.maximum(v[i], v[i + stride]), jnp.minimum(v[i], v[i + stride])
            stride //= 2
        shift //= 2
    return v, dropped


def _top_values(tiles):
    v = list(tiles)
    assert len(v) == PEER_TOPK
    for i, j in _sort_network(len(v)):
        v[i], v[j] = jnp.maximum(v[i], v[j]), jnp.minimum(v[i], v[j])
    top, dropped = _merge_columns(v)
    return top + [_all_sublanes(jnp.maximum, dropped)]


def _pair_threshold(a, b):
    sub = lax.broadcasted_iota(jnp.int32, (SUBLANES, LANES), 0)
    b0 = b[SUBLANES - 1]
    for s in reversed(range(SUBLANES - 1)):
        b0 = jnp.where(sub == s, b[s], b0)
    top, dropped = _merge_columns([b0 + a[k] for k in range(PEER_TOPK)])
    t17 = _all_sublanes(jnp.maximum, dropped)
    x = a[PEER_TOPK] + b[0]
    extra = []
    for l in range(SUBLANES, N_TOP):
        y = a[0] + b[l]
        extra.append(jnp.maximum(y, x))
        x = jnp.minimum(y, x)
    extra.append(x)
    sel = list(top)
    for j, e in enumerate(extra):
        k = PEER_TOPK - 1 - j
        t17 = jnp.maximum(t17, jnp.minimum(top[k], e))
        sel[k] = jnp.maximum(top[k], e)
    v16 = functools.reduce(jnp.minimum, sel[PEER_TOPK - 1 - len(extra):])
    return sel, v16, t17


def _peer_kernel(xn_ref, x_ref, wq_ref, keys_ref, u_ref, unext_ref, vt_ref, out_ref,
                 tau_sc, e0_sc, s1_sc, e1_sc, s0_sc, acc_sc,
                 sa_sc, sb_sc, wa_sc, wb_sc, vprev_sc, trow_sc, erow_sc, qt_sc, *, n_ph):
    et = pl.program_id(1)
    tb = xn_ref.shape[0]
    nk = keys_ref.shape[1]
    phalf = keys_ref.shape[2]
    e_tile = u_ref.shape[0]

    @pl.when(et == 0)
    def _():
        acc_sc[...] = jnp.zeros_like(acc_sc)
        wb_sc[...] = jnp.zeros_like(wb_sc)
        vprev_sc[...] = jnp.zeros_like(vprev_sc)
        xn = xn_ref[...]
        q_rows = wq_ref.shape[0] // 4
        for c in range(4):
            rows = slice(c * q_rows, (c + 1) * q_rows)
            qt_sc[rows, :] = lax.dot_general(wq_ref[rows, :], xn, NT_DIMS,
                                             preferred_element_type=F32).astype(BF16)

        def head(h, carry):
            for c, dst in ((0, s0_sc), (1, s1_sc)):
                hc = 2 * h + c
                q_t = qt_sc[pl.ds(pl.multiple_of(hc * phalf, phalf), phalf), :]
                dst[...] = jnp.dot(keys_ref[hc], q_t, preferred_element_type=F32)
            for lc in range(tb // LANES):
                lanes = slice(lc * LANES, (lc + 1) * LANES)
                rows = [slice(k * SUBLANES, (k + 1) * SUBLANES) for k in range(nk // SUBLANES)]
                s0 = [s0_sc[r, lanes] for r in rows]
                s1 = [s1_sc[r, lanes] for r in rows]
                a = _top_values(s0)
                b = _top_values(s1)
                sel, v16, v17 = _pair_threshold(a, b)
                tau = 0.5 * (v16 + v17)
                z = functools.reduce(jnp.add, [jnp.exp(t - sel[0]) for t in sel])
                rz = 0.5 / z
                tb1 = tau - b[0]
                for r, t0, t1 in zip(rows, s0, s1):
                    tau_sc[h, r, lanes] = jnp.exp(tb1 - t0)
                    e0_sc[h, r, lanes] = jnp.exp(t0 - a[0]) * rz
                    e1_sc[h, r, lanes] = jnp.exp(t1 - b[0])
            return carry

        lax.fori_loop(0, n_ph, head, 0)
        sa_sc[...] = lax.dot_general(u_ref[0:e_tile // 2, :], xn, NT_DIMS, preferred_element_type=F32)

    half = e_tile // 2

    base = pl.multiple_of(et * 2 * SUBLANES, 2 * SUBLANES)
    for h in range(n_ph):
        trow_sc[h] = tau_sc[h, pl.ds(base, 2 * SUBLANES), :]
        erow_sc[h] = e0_sc[h, pl.ds(base, 2 * SUBLANES), :]

    def gates(tile, s_in, w_out, lc, anchor=None):
        rows = slice(tile * SUBLANES, (tile + 1) * SUBLANES)
        lanes = slice(lc * LANES, (lc + 1) * LANES)
        n_jt = nk // SUBLANES
        last = None
        for r in range(SUBLANES):
            g = [None] * n_jt
            for h in range(n_ph):
                t0 = jnp.broadcast_to(trow_sc[h, rows, lanes][r:r + 1], (SUBLANES, LANES))
                ee = jnp.broadcast_to(erow_sc[h, rows, lanes][r:r + 1], (SUBLANES, LANES))
                for jt in range(n_jt):
                    x1 = e1_sc[h, jt * SUBLANES:(jt + 1) * SUBLANES, lanes]
                    term = jnp.where(x1 >= t0, x1, 0.0) * ee
                    g[jt] = term if g[jt] is None else g[jt] + term
            for jp in range(n_jt // 2):
                r0 = r * nk + jp * 2 * SUBLANES
                w_pair = []
                for jj in range(2):
                    sv = s_in[r0 + jj * SUBLANES:r0 + (jj + 1) * SUBLANES, lanes]
                    if anchor is not None:
                        sv, anchor = sv + anchor, None
                    inner = sv * (GELU_C1 + GELU_C2 * (sv * sv))
                    w_tile = g[2 * jp + jj] * (sv + sv * jnp.tanh(inner))
                    w_pair.append(w_tile)
                    last = w_tile if last is None else jnp.maximum(last, w_tile)
                w_out[r0:r0 + 2 * SUBLANES, lanes] = jnp.concatenate(w_pair, axis=0).astype(BF16)
        return last

    d_half = acc_sc.shape[0] // 2
    q_rows = half // 2

    def held(operand, after):
        if after is None:
            return operand
        return operand + _exact_zero(after)[0:1, 0:1].astype(operand.dtype)

    def acc_piece(v_ref, w_ref, p, after):
        rows = slice(p * d_half, (p + 1) * d_half)
        res = acc_sc[rows, :] + jnp.dot(v_ref[rows, :], held(w_ref[...], after),
                                        preferred_element_type=F32)
        acc_sc[rows, :] = res
        return res[-SUBLANES:, -LANES:]

    def score_piece(s_ref, rows_ref, rows, after):
        res = lax.dot_general(rows_ref[rows, :], held(xn_ref[...], after), NT_DIMS,
                              preferred_element_type=F32)
        s_ref[rows, :] = res
        return res[-SUBLANES:, -LANES:]

    ub_ref = u_ref.at[half:, :]
    va_ref = vt_ref.at[:, 0:half]
    lo_rows, hi_rows, all_rows = slice(0, q_rows), slice(q_rows, half), slice(0, half)
    mxu_work = [lambda a: acc_piece(vprev_sc, wb_sc, 0, a), lambda a: acc_piece(vprev_sc, wb_sc, 1, a),
                lambda a: score_piece(sb_sc, ub_ref, lo_rows, a), lambda a: score_piece(sb_sc, ub_ref, hi_rows, a),
                lambda a: acc_piece(va_ref, wa_sc, 0, a), lambda a: acc_piece(va_ref, wa_sc, 1, a),
                lambda a: score_piece(sa_sc, unext_ref, all_rows, a)]
    n_lc = tb // LANES
    n_sec = 2 * n_lc
    first_sec = (0, 0, 0, 1, n_lc, n_lc, n_lc)
    anchored = n_sec > len(mxu_work)
    sec_out, piece_out = [], []
    for sec in range(n_sec):
        tile, lc = divmod(sec, n_lc)
        anchor = None
        for k in range(len(mxu_work)):
            if (k if anchored else first_sec[k]) == sec:
                after = sec_out[sec - 2] if anchored and sec >= 2 else None
                piece_out.append(mxu_work[k](after))
        if anchored and 1 <= sec <= len(mxu_work):
            anchor = _exact_zero(piece_out[sec - 1])
        sec_out.append(gates(tile, (sa_sc, sb_sc)[tile], (wa_sc, wb_sc)[tile], lc, anchor))
    vprev_sc[...] = vt_ref[:, half:]

    @pl.when(et == pl.num_programs(1) - 1)
    def _():
        tail = jnp.dot(vprev_sc[...], wb_sc[...], preferred_element_type=F32)
        out_ref[...] = x_ref[...] + (acc_sc[...] + tail).T


def _peer(xn2, x2d, wq_t, keys, u_b, v_t):
    t, d = x2d.shape
    n_hc, nk, phalf = keys.shape
    n_ph = n_hc // 2
    n_exp = u_b.shape[0]
    tb = _pick(t, (512, 256))
    half = SUBLANES * nk
    e_tile = 2 * half
    assert n_exp % e_tile == 0 and tb % MXU_COLS == 0
    n_steps = n_exp // e_tile
    kern = functools.partial(_peer_kernel, n_ph=n_ph)
    head_buf = lambda: pltpu.VMEM((n_ph, nk, tb), F32)
    once = dict(pipeline_mode=pl.Buffered(1))
    return pl.pallas_call(
        kern,
        grid=(t // tb, n_exp // e_tile),
        in_specs=[pl.BlockSpec((tb, d), lambda i, e: (i, 0), **once),
                  pl.BlockSpec((tb, d), lambda i, e: (i, 0), **once),
                  pl.BlockSpec(wq_t.shape, lambda i, e: (0, 0), **once),
                  pl.BlockSpec(keys.shape, lambda i, e: (0, 0, 0), **once),
                  pl.BlockSpec((e_tile, d), lambda i, e: (e, 0)),
                  pl.BlockSpec((half, d), lambda i, e: (jnp.minimum(2 * e + 2, 2 * n_steps - 2), 0)),
                  pl.BlockSpec((d, e_tile), lambda i, e: (0, e))],
        out_specs=pl.BlockSpec((tb, d), lambda i, e: (i, 0)),
        out_shape=jax.ShapeDtypeStruct((t, d), F32),
        scratch_shapes=[head_buf(), head_buf(), pltpu.VMEM((nk, tb), F32), head_buf(),
                        pltpu.VMEM((nk, tb), F32),
                        pltpu.VMEM((d, tb), F32),
                        pltpu.VMEM((half, tb), F32), pltpu.VMEM((half, tb), F32),
                        pltpu.VMEM((half, tb), BF16), pltpu.VMEM((half, tb), BF16),
                        pltpu.VMEM((d, half), BF16),
                        pltpu.VMEM((n_ph, 2 * SUBLANES, tb), F32), pltpu.VMEM((n_ph, 2 * SUBLANES, tb), F32),
                        pltpu.VMEM((wq_t.shape[0], tb), BF16)],
        compiler_params=_params(("parallel", "arbitrary")),
        name="peer",
    )(xn2, x2d, wq_t, keys, u_b, u_b, v_t)


def _norm_kernel(x_ref, w_ref, o_ref):
    o_ref[...] = _rms(x_ref[...], w_ref[...])


def _final_norm(x2d, w):
    t, d = x2d.shape
    tm = _pick(t, (1024, 512, 256, 128))
    return pl.pallas_call(
        _norm_kernel,
        grid=(t // tm,),
        in_specs=[pl.BlockSpec((tm, d), lambda i: (i, 0)), pl.BlockSpec((1, d), lambda i: (0, 0))],
        out_specs=pl.BlockSpec((tm, d), lambda i: (i, 0)),
        out_shape=jax.ShapeDtypeStruct((t, d), F32),
        compiler_params=_params(("parallel",)),
        name="final_norm",
    )(x2d, w.reshape(1, d))


def _prep_layer(l, w, dims):
    m_dim, n_mh, s_dim, cd, n_sh, d = dims
    w_in = w["w_in"][l]
    o = [0]
    def take(n):
        o[0] += n
        return w_in[:, o[0] - n:o[0]]
    qkvo, w_ig, w_fg = take(4 * m_dim), take(n_mh), take(n_mh)
    w_z, w_xbc, w_dt, w_g = take(s_dim), take(cd), take(n_sh), take(2 * d)
    n_small = 2 * n_mh + n_sh
    w_small = jnp.concatenate([w_ig, w_fg, w_dt, jnp.zeros((d, SMALL_W - n_small), F32)], axis=1)
    b_small = jnp.concatenate([w["b_igate"][l], w["b_fgate"][l], w["dt_bias"][l],
                               jnp.zeros((SMALL_W - n_small,), F32)]).reshape(1, SMALL_W)
    keys = w["peer_keys"][l]
    return dict(
        norm1=w["norm1_w"][l].reshape(1, d),
        w_big=jnp.concatenate([qkvo, w_xbc, w_z, w_g], axis=1).astype(BF16),
        w_small=w_small.astype(BF16), b_small=b_small,
        w_a=w["w_a"][l].astype(BF16), w_b=w["w_b"][l].astype(BF16), w_out=w["w_out"][l].astype(BF16),
        norm2=w["norm2_w"][l].reshape(1, d),
        wq_t=w["peer_wq"][l].T.astype(BF16),
        keys=keys.reshape((keys.shape[0] * 2,) + keys.shape[2:]).astype(BF16),
        u_b=w["peer_u"][l].astype(BF16), v_t=w["peer_v"][l].T.astype(BF16),
    )


def _trunk(x, c0, n0, m0, ssm0, conv0, w, prepped, dims):
    m_dim, n_mh, s_dim, cd, n_sh, d = dims
    b, s, _ = x.shape
    x2 = x.reshape(b * s, d)
    outs = [[] for _ in range(5)]
    for l, p in enumerate(prepped):
        big2, small2 = _in_proj(x2, p["norm1"], p["w_big"], p["w_small"], p["b_small"])
        big3 = big2.reshape(b, s, -1)
        small3 = small2.reshape(b, s, SMALL_W)
        small_t = small3.transpose(0, 2, 1)
        hm, c, n, m = _mlstm(big3, small_t, c0[l], n0[l], m0[l], w["mlstm_norm_w"][l])
        ys, ssm, conv = _ssd(big3, small3, small_t, conv0[l], ssm0[l], w["conv_w"][l], w["conv_b"][l],
                             w["a_log"][l], w["d_skip"][l], w["ssm_norm_w"][l], m_dim=m_dim, n_mh=n_mh)
        ga_blk = (4 * m_dim + cd + s_dim) // d
        x1, xn2 = _out_proj(x2, hm.reshape(b * s, -1), ys.reshape(b * s, -1), big2,
                            p["w_a"], p["w_b"], p["w_out"], p["norm2"], ga_blk)
        x2 = _peer(xn2, x1, p["wq_t"], p["keys"], p["u_b"], p["v_t"])
        for lst, val in zip(outs, (c, n.reshape(b, n_mh, -1), m.reshape(b, n_mh), ssm, conv)):
            lst.append(val)
    y = _final_norm(x2, w["final_norm_w"]).reshape(b, s, d)
    return (y,) + tuple(jnp.stack(o) for o in outs)


def kernel(x_prompt, x_sample, state_mlstm_C, state_mlstm_n, state_mlstm_m, state_ssm, cache_conv,
           norm1_w, w_in, b_igate, b_fgate, mlstm_norm_w, conv_w, conv_b, dt_bias, a_log, d_skip,
           ssm_norm_w, w_a, w_b, w_out, norm2_w, peer_wq, peer_keys, peer_u, peer_v, final_norm_w):
    w = dict(norm1_w=norm1_w, w_in=w_in, b_igate=b_igate, b_fgate=b_fgate, mlstm_norm_w=mlstm_norm_w,
             conv_w=conv_w, conv_b=conv_b, dt_bias=dt_bias, a_log=a_log, d_skip=d_skip,
             ssm_norm_w=ssm_norm_w, w_a=w_a, w_b=w_b, w_out=w_out, norm2_w=norm2_w, peer_wq=peer_wq,
             peer_keys=peer_keys, peer_u=peer_u, peer_v=peer_v, final_norm_w=final_norm_w)
    depth, _, n_mh, dh, _ = state_mlstm_C.shape
    n_sh, pdim = state_ssm.shape[2], state_ssm.shape[3]
    d = x_prompt.shape[-1]
    dims = (n_mh * dh, n_mh, n_sh * pdim, cache_conv.shape[-1], n_sh, d)
    prepped = [_prep_layer(l, w, dims) for l in range(depth)]
    bp = x_prompt.shape[0]
    zeros = lambda a: jnp.zeros((depth, bp) + a.shape[2:], F32)
    yp = _trunk(x_prompt, zeros(state_mlstm_C), zeros(state_mlstm_n), zeros(state_mlstm_m),
                zeros(state_ssm), zeros(cache_conv), w, prepped, dims)
    ys = _trunk(x_sample, state_mlstm_C, state_mlstm_n, state_mlstm_m, state_ssm, cache_conv,
                w, prepped, dims)
    return (yp[0], ys[0]) + yp[1:] + ys[1:]
```

```python
import functools

import jax
import jax.numpy as jnp
from jax import lax
from jax.experimental import pallas as pl
from jax.experimental.pallas import tpu as pltpu

F32 = jnp.float32
BF16 = jnp.bfloat16
EPS = 1e-6
CONV_W = 4
PEER_TOPK = 16
LANES = 128
SUBLANES = 8
MXU_COLS = 256
SMALL_W = 128
VMEM_LIMIT = 56 * 1024 * 1024
NT_DIMS = (((1,), (1,)), ((), ()))


def _pick(n, prefs):
    for p in prefs:
        if n % p == 0:
            return p
    return n


def _softplus(x):
    return jnp.maximum(x, 0.0) + jnp.log1p(jnp.exp(-jnp.abs(x)))


def _log_sigmoid(x):
    return jnp.minimum(x, 0.0) - jnp.log1p(jnp.exp(-jnp.abs(x)))


def _sigmoid(x):
    return 0.5 * jnp.tanh(0.5 * x) + 0.5


def _silu(x):
    h = 0.5 * x
    return h + h * jnp.tanh(h)


def _split3(x):
    hi = x.astype(BF16)
    r1 = x - hi.astype(F32)
    mid = r1.astype(BF16)
    lo = (r1 - mid.astype(F32)).astype(BF16)
    return hi, mid, lo


def _exact_zero(tile):
    bits = pltpu.bitcast(tile, jnp.uint32)
    sixteen = jnp.uint32(16)
    return pltpu.bitcast(lax.shift_right_logical(lax.shift_right_logical(bits, sixteen), sixteen), F32)


def _rms(x, w):
    return x * lax.rsqrt(jnp.mean(x * x, axis=-1, keepdims=True) + EPS) * w


def _params(sem):
    return pltpu.CompilerParams(dimension_semantics=sem, vmem_limit_bytes=VMEM_LIMIT)


def _in_proj_kernel(x_ref, nw_ref, wbig_ref, wsmall_ref, bsmall_ref, big_ref, small_ref, h_sc):
    @pl.when(pl.program_id(1) == 0)
    def _():
        hb = _rms(x_ref[...], nw_ref[...]).astype(BF16)
        h_sc[...] = hb
        small_ref[...] = jnp.dot(hb, wsmall_ref[...], preferred_element_type=F32) + bsmall_ref[...]

    big_ref[...] = jnp.dot(h_sc[...], wbig_ref[...], preferred_element_type=F32).astype(BF16)


def _in_proj(x2d, nw, w_big, w_small, b_small):
    t, d = x2d.shape
    nbig = w_big.shape[1]
    tm = _pick(t, (1024, 512, 256, 128))
    tn = _pick(nbig, (2048, 1536, 1024, 512, 256, 128))
    return pl.pallas_call(
        _in_proj_kernel,
        grid=(t // tm, nbig // tn),
        in_specs=[
            pl.BlockSpec((tm, d), lambda i, j: (i, 0)),
            pl.BlockSpec((1, d), lambda i, j: (0, 0)),
            pl.BlockSpec((d, tn), lambda i, j: (0, j)),
            pl.BlockSpec((d, SMALL_W), lambda i, j: (0, 0)),
            pl.BlockSpec((1, SMALL_W), lambda i, j: (0, 0)),
        ],
        out_specs=[
            pl.BlockSpec((tm, tn), lambda i, j: (i, j)),
            pl.BlockSpec((tm, SMALL_W), lambda i, j: (i, 0)),
        ],
        out_shape=[jax.ShapeDtypeStruct((t, nbig), BF16), jax.ShapeDtypeStruct((t, SMALL_W), F32)],
        scratch_shapes=[pltpu.VMEM((tm, d), BF16)],
        compiler_params=_params(("parallel", "arbitrary")),
        name="in_proj",
    )(x2d, nw, w_big, w_small, b_small)


def _mlstm_kernel(q_ref, k_ref, v_ref, og_ref, g_ref, c0_ref, n0_ref, m0_ref, nw_ref,
                  h_ref, c_ref, n_ref, m_ref, c_sc, n_sc, m_sc, *, n_heads, scale):
    ci = pl.program_id(1)
    lc = q_ref.shape[1]
    dh = c_sc.shape[2]

    @pl.when(ci == 0)
    def _():
        c_sc[...] = c0_ref[...]
        n_sc[...] = n0_ref[...]
        m_sc[...] = m0_ref[...]

    row = lax.broadcasted_iota(jnp.int32, (lc, lc), 0)
    col = lax.broadcasted_iota(jnp.int32, (lc, lc), 1)
    causal = col <= row
    tri = causal.astype(BF16)
    triu = (row <= col).astype(BF16)
    for bb, hh in [(bb, hh) for bb in range(q_ref.shape[0]) for hh in range(n_heads)]:
        gates = g_ref[bb]
        cols = slice(hh * dh, (hh + 1) * dh)
        ig = gates[hh:hh + 1, :]
        lf = _log_sigmoid(gates[n_heads + hh:n_heads + hh + 1, :])
        lf_parts = _split3(jnp.broadcast_to(lf, (SUBLANES, lc)))
        fcum_col = sum(lax.dot_general(tri, p, NT_DIMS, preferred_element_type=F32)
                       for p in lf_parts)[:, 0:1]
        fcum_row = sum(jnp.dot(p, triu, preferred_element_type=F32) for p in lf_parts)[0:1, :]

        m_prev = m_sc[bb, hh]
        logw = jnp.where(causal, fcum_col - fcum_row + ig, -jnp.inf)
        log_prev = fcum_col + m_prev
        m_t = jnp.maximum(log_prev, jnp.max(logw, axis=1, keepdims=True))
        a_prev = jnp.exp(log_prev - m_t)
        decay = jnp.exp(logw - m_t)

        q = q_ref[bb, :, cols]
        k = k_ref[bb, :, cols]
        v = v_ref[bb, :, cols]
        s = lax.dot_general(q, k, NT_DIMS, preferred_element_type=F32) * scale * decay
        num = (jnp.dot(s.astype(BF16), v, preferred_element_type=F32)
               + a_prev * jnp.dot(q, c_sc[bb, hh].astype(BF16), preferred_element_type=F32))
        qn = jnp.sum(q.astype(F32) * n_sc[bb, hh], axis=1, keepdims=True)
        den = jnp.sum(s, axis=1, keepdims=True) + a_prev * qn
        hout = num / jnp.maximum(jnp.abs(den), jnp.exp(-m_t))
        y = _rms(hout, nw_ref[hh]) * _sigmoid(og_ref[bb, :, cols].astype(F32))
        h_ref[bb, :, cols] = y.astype(BF16)

        f_end = fcum_row[:, lc - 1:lc]
        m_new = m_t[lc - 1:lc, :]
        w_row = jnp.exp(f_end - fcum_row + ig - m_new) * scale
        a_end = jnp.exp(f_end + m_prev - m_new)
        kw = (k.astype(F32).T * w_row).astype(BF16)
        c_sc[bb, hh] = a_end * c_sc[bb, hh] + jnp.dot(kw, v, preferred_element_type=F32)
        w8 = jnp.broadcast_to(w_row, (SUBLANES, lc)).astype(BF16)
        n_sc[bb, hh] = a_end * n_sc[bb, hh] + jnp.dot(w8, k, preferred_element_type=F32)[0:1, :]
        m_sc[bb, hh] = m_new

    @pl.when(ci == pl.num_programs(1) - 1)
    def _():
        c_ref[...] = c_sc[...]
        n_ref[...] = n_sc[...]
        m_ref[...] = m_sc[...]


def _mlstm(big3, small_t, c0, n0, m0, norm_w):
    b, s, _ = big3.shape
    _, nh, dh, _ = c0.shape
    m_dim = nh * dh
    lc = _pick(s, (256, 128, 64, 32, 16))
    bg = _pick(b, (2, 1))
    kern = functools.partial(_mlstm_kernel, n_heads=nh, scale=float(dh) ** -0.5)
    blk = lambda off: pl.BlockSpec((bg, lc, m_dim), lambda bi, ci: (bi, ci, off))
    st4 = lambda r, c: pl.BlockSpec((bg, nh, r, c), lambda bi, ci: (bi, 0, 0, 0))
    return pl.pallas_call(
        kern,
        grid=(b // bg, s // lc),
        in_specs=[blk(0), blk(1), blk(2), blk(3),
                  pl.BlockSpec((bg, SUBLANES, lc), lambda bi, ci: (bi, 0, ci)),
                  st4(dh, dh), st4(1, dh), st4(1, 1),
                  pl.BlockSpec((nh, 1, dh), lambda bi, ci: (0, 0, 0))],
        out_specs=[pl.BlockSpec((bg, lc, m_dim), lambda bi, ci: (bi, ci, 0)),
                   st4(dh, dh), st4(1, dh), st4(1, 1)],
        out_shape=[jax.ShapeDtypeStruct((b, s, m_dim), BF16),
                   jax.ShapeDtypeStruct((b, nh, dh, dh), F32),
                   jax.ShapeDtypeStruct((b, nh, 1, dh), F32),
                   jax.ShapeDtypeStruct((b, nh, 1, 1), F32)],
        scratch_shapes=[pltpu.VMEM((bg, nh, dh, dh), F32), pltpu.VMEM((bg, nh, 1, dh), F32),
                        pltpu.VMEM((bg, nh, 1, 1), F32)],
        compiler_params=_params(("parallel", "arbitrary")),
        name="mlstm",
    )(big3, big3, big3, big3, small_t, c0, n0.reshape(b, nh, 1, dh), m0.reshape(b, nh, 1, 1),
      norm_w.reshape(nh, 1, dh))


def _ssd_kernel(xbc_ref, z_ref, sm_ref, smt_ref, conv0_ref, ssm0_ref, cw_ref, cb_ref,
                alr_ref, alc_ref, dsk_ref, nw_ref, ex_ref,
                ys_ref, ssm_ref, conv_ref, xpad, ht_sc,
                *, dt_off, n_sh, n_groups, n_state, s_dim):
    ci = pl.program_id(1)
    lc = xbc_ref.shape[1]
    gw = s_dim // n_groups
    hpg = n_sh // n_groups
    pdim = gw // hpg
    pad = SUBLANES

    @pl.when(ci == 0)
    def _():
        xpad[pad - (CONV_W - 1):pad, :] = conv0_ref[0]
        for g in range(n_groups):
            ht_sc[g] = ssm0_ref[0, g].T

    xpad[pad:pad + lc, :] = xbc_ref[0].astype(F32)
    conv = cb_ref[...]
    for i in range(CONV_W):
        conv = conv + xpad[pad - (CONV_W - 1) + i:pad - (CONV_W - 1) + i + lc, :] * cw_ref[i:i + 1, :]
    tail = xpad[pad + lc - (CONV_W - 1):pad + lc, :]
    xpad[pad - (CONV_W - 1):pad, :] = tail
    act = _silu(conv)
    xs = act[:, :s_dim]
    bm = act[:, s_dim:s_dim + n_groups * n_state]
    cm = act[:, s_dim + n_groups * n_state:]

    dt_col = _softplus(sm_ref[0][:, dt_off:dt_off + n_sh])
    dt_row = _softplus(smt_ref[0][dt_off:dt_off + n_sh, :])
    la_col = dt_col * (-jnp.exp(alr_ref[...]))
    la_row = dt_row * (-jnp.exp(alc_ref[...]))
    row = lax.broadcasted_iota(jnp.int32, (lc, lc), 0)
    col = lax.broadcasted_iota(jnp.int32, (lc, lc), 1)
    causal = col <= row
    tri = causal.astype(BF16)
    triu = (row <= col).astype(BF16)
    cum_col = sum(jnp.dot(tri, p, preferred_element_type=F32) for p in _split3(la_col))
    cum_row = sum(jnp.dot(p, triu, preferred_element_type=F32) for p in _split3(la_row))
    w_end = jnp.exp(cum_col[lc - 1:lc, :] - cum_col) * dt_col
    ex3 = ex_ref[...]
    ecum_x = jnp.dot(jnp.concatenate(_split3(jnp.exp(cum_col)), axis=1), ex3, preferred_element_type=F32)
    wend_x = jnp.dot(jnp.concatenate(_split3(w_end), axis=1), ex3, preferred_element_type=F32)
    xw = (xs * wend_x).astype(BF16)
    xs_b = xs.astype(BF16)
    head_of_lane = lax.div(lax.broadcasted_iota(jnp.int32, (1, gw), 1), pdim)

    parts = []
    for g in range(n_groups):
        bm_g = bm[:, g * n_state:(g + 1) * n_state]
        cm_b = cm[:, g * n_state:(g + 1) * n_state].astype(BF16)
        cb = lax.dot_general(cm_b, bm_g.astype(BF16), NT_DIMS, preferred_element_type=F32)
        xg = xs_b[:, g * gw:(g + 1) * gw]
        yg = (jnp.dot(cm_b, ht_sc[g].astype(BF16), preferred_element_type=F32)
              * ecum_x[:, g * gw:(g + 1) * gw])
        for kk in range(hpg):
            hd = g * hpg + kk
            dec = jnp.exp(jnp.where(causal, cum_col[:, hd:hd + 1] - cum_row[hd:hd + 1, :], -jnp.inf))
            wm = (cb * dec * dt_row[hd:hd + 1, :]).astype(BF16)
            xm = jnp.where(head_of_lane == kk, xg, jnp.zeros_like(xg))
            yg = yg + jnp.dot(wm, xm, preferred_element_type=F32)
        ht_sc[g] = (ecum_x[lc - 1:lc, g * gw:(g + 1) * gw] * ht_sc[g]
                    + jnp.dot(bm_g.T.astype(BF16), xw[:, g * gw:(g + 1) * gw], preferred_element_type=F32))
        parts.append(yg)
    y = jnp.concatenate(parts, axis=1) + dsk_ref[...] * xs
    zf = z_ref[0].astype(F32)
    y = y * _silu(zf)
    ys_ref[0] = _rms(y, nw_ref[...]).astype(BF16)

    @pl.when(ci == pl.num_programs(1) - 1)
    def _():
        conv_ref[0] = tail
        for g in range(n_groups):
            ssm_ref[0, g] = ht_sc[g].T


def _ssd(big3, small3, small_t, conv0, ssm0, conv_w, conv_b, a_log, d_skip, norm_w, *, m_dim, n_mh):
    b, s, _ = big3.shape
    _, n_sh, pdim, n_state = ssm0.shape
    cd = conv0.shape[2]
    s_dim = n_sh * pdim
    n_groups = (cd - s_dim) // (2 * n_state)
    gw = s_dim // n_groups
    lc = _pick(s, (256, 128, 64, 32, 16))
    xbc_blk = (4 * m_dim) // cd
    z_blk = (4 * m_dim + cd) // s_dim
    assert xbc_blk * cd == 4 * m_dim and z_blk * s_dim == 4 * m_dim + cd
    kern = functools.partial(_ssd_kernel, dt_off=2 * n_mh, n_sh=n_sh, n_groups=n_groups,
                             n_state=n_state, s_dim=s_dim)
    full2 = lambda r, c: pl.BlockSpec((r, c), lambda bi, ci: (0, 0))
    expander = jnp.tile(jnp.repeat(jnp.eye(n_sh, dtype=BF16), pdim, axis=1), (3, 1))
    ys, ssm, conv = pl.pallas_call(
        kern,
        grid=(b, s // lc),
        in_specs=[pl.BlockSpec((1, lc, cd), lambda bi, ci: (bi, ci, xbc_blk)),
                  pl.BlockSpec((1, lc, s_dim), lambda bi, ci: (bi, ci, z_blk)),
                  pl.BlockSpec((1, lc, SMALL_W), lambda bi, ci: (bi, ci, 0)),
                  pl.BlockSpec((1, SMALL_W, lc), lambda bi, ci: (bi, 0, ci)),
                  pl.BlockSpec((1, CONV_W - 1, cd), lambda bi, ci: (bi, 0, 0)),
                  pl.BlockSpec((1, n_groups, gw, n_state), lambda bi, ci: (bi, 0, 0, 0)),
                  full2(CONV_W, cd), full2(1, cd), full2(1, n_sh), full2(n_sh, 1),
                  full2(1, s_dim), full2(1, s_dim), full2(3 * n_sh, s_dim)],
        out_specs=[pl.BlockSpec((1, lc, s_dim), lambda bi, ci: (bi, ci, 0)),
                   pl.BlockSpec((1, n_groups, gw, n_state), lambda bi, ci: (bi, 0, 0, 0)),
                   pl.BlockSpec((1, CONV_W - 1, cd), lambda bi, ci: (bi, 0, 0))],
        out_shape=[jax.ShapeDtypeStruct((b, s, s_dim), BF16),
                   jax.ShapeDtypeStruct((b, n_groups, gw, n_state), F32),
                   jax.ShapeDtypeStruct((b, CONV_W - 1, cd), F32)],
        scratch_shapes=[pltpu.VMEM((lc + SUBLANES, cd), F32), pltpu.VMEM((n_groups, n_state, gw), F32)],
        compiler_params=_params(("parallel", "arbitrary")),
        name="ssd",
    )(big3, big3, small3, small_t, conv0, ssm0.reshape(b, n_groups, gw, n_state), conv_w,
      conv_b.reshape(1, cd), a_log.reshape(1, n_sh), a_log.reshape(n_sh, 1),
      jnp.repeat(d_skip, pdim).reshape(1, s_dim), norm_w.reshape(1, s_dim), expander)
    return ys, ssm.reshape(b, n_sh, pdim, n_state), conv


def _out_kernel(x_ref, hm_ref, ys_ref, ga_ref, gb_ref, wa_ref, wb_ref, wo_ref, n2_ref, xo_ref, xn_ref):
    a = jnp.dot(hm_ref[...], wa_ref[...], preferred_element_type=F32)
    bb = jnp.dot(ys_ref[...], wb_ref[...], preferred_element_type=F32)
    mix = _sigmoid(ga_ref[...].astype(F32)) * a + _sigmoid(gb_ref[...].astype(F32)) * bb
    xo = x_ref[...] + jnp.dot(mix.astype(BF16), wo_ref[...], preferred_element_type=F32)
    xo_ref[...] = xo
    xn_ref[...] = _rms(xo, n2_ref[...]).astype(BF16)


def _out_proj(x2d, hm2, ys2, big2, w_a, w_b, w_out, norm2_w, ga_blk):
    t, d = x2d.shape
    tm = _pick(t, (512, 256, 128))
    full = lambda a: pl.BlockSpec(a.shape, lambda i: (0, 0))
    rows = lambda w: pl.BlockSpec((tm, w), lambda i: (i, 0))
    return pl.pallas_call(
        _out_kernel,
        grid=(t // tm,),
        in_specs=[rows(d), rows(hm2.shape[1]), rows(ys2.shape[1]),
                  pl.BlockSpec((tm, d), lambda i: (i, ga_blk)),
                  pl.BlockSpec((tm, d), lambda i: (i, ga_blk + 1)),
                  full(w_a), full(w_b), full(w_out), full(norm2_w)],
        out_specs=[rows(d), rows(d)],
        out_shape=[jax.ShapeDtypeStruct((t, d), F32), jax.ShapeDtypeStruct((t, d), BF16)],
        compiler_params=_params(("parallel",)),
        name="out_proj",
    )(x2d, hm2, ys2, big2, big2, w_a, w_b, w_out, norm2_w)


N_TOP = PEER_TOPK + 1
GELU_C1 = (2.0 / 3.141592653589793) ** 0.5
GELU_C2 = GELU_C1 * 0.044715


def _sort_network(n):
    def merge(lo, hi, r):
        step = 2 * r
        if step < hi - lo:
            yield from merge(lo, hi, step)
            yield from merge(lo + r, hi, step)
            yield from ((i, i + r) for i in range(lo + r, hi - r, step))
        else:
            yield (lo, lo + r)

    def sort(lo, hi):
        if hi > lo:
            mid = lo + (hi - lo) // 2
            yield from sort(lo, mid)
            yield from sort(mid + 1, hi)
            yield from merge(lo, hi, 1)

    return tuple(sort(0, n - 1))


def _all_sublanes(op, x):
    shift = SUBLANES // 2
    while shift:
        x = op(x, pltpu.roll(x, shift, 0))
        shift //= 2
    return x


def _merge_columns(v):
    n = len(v)
    dropped = None
    shift = SUBLANES // 2
    while shift:
        partner = [pltpu.roll(v[n - 1 - k], shift, 0) for k in range(n)]
        lo = functools.reduce(jnp.maximum, [jnp.minimum(v[k], partner[k]) for k in range(n)])
        dropped = lo if dropped is None else jnp.maximum(dropped, lo)
        v = [jnp.maximum(v[k], partner[k]) for k in range(n)]
        stride = n // 2
        while stride:
            for i in range(n):
                if not i & stride:
                    v[i], v[i + stride] = jnp.maximum(v[i], v[i + stride]), jnp.minimum(v[i], v[i + stride])
            stride //= 2
        shift //= 2
    return v, dropped


def _top_values(tiles):
    v = list(tiles)
    assert len(v) == PEER_TOPK
    for i, j in _sort_network(len(v)):
        v[i], v[j] = jnp.maximum(v[i], v[j]), jnp.minimum(v[i], v[j])
    top, dropped = _merge_columns(v)
    return top + [_all_sublanes(jnp.maximum, dropped)]


def _pair_threshold(a, b):
    sub = lax.broadcasted_iota(jnp.int32, (SUBLANES, LANES), 0)
    b0 = b[SUBLANES - 1]
    for s in reversed(range(SUBLANES - 1)):
        b0 = jnp.where(sub == s, b[s], b0)
    top, dropped = _merge_columns([b0 + a[k] for k in range(PEER_TOPK)])
    t17 = _all_sublanes(jnp.maximum, dropped)
    x = a[PEER_TOPK] + b[0]
    extra = []
    for l in range(SUBLANES, N_TOP):
        y = a[0] + b[l]
        extra.append(jnp.maximum(y, x))
        x = jnp.minimum(y, x)
    extra.append(x)
    sel = list(top)
    for j, e in enumerate(extra):
        k = PEER_TOPK - 1 - j
        t17 = jnp.maximum(t17, jnp.minimum(top[k], e))
        sel[k] = jnp.maximum(top[k], e)
    v16 = functools.reduce(jnp.minimum, sel[PEER_TOPK - 1 - len(extra):])
    return sel, v16, t17


def _peer_kernel(xn_ref, x_ref, wq_ref, keys_ref, u_ref, unext_ref, vt_ref, out_ref,
                 tau_sc, e0_sc, s1_sc, e1_sc, s0_sc, acc_sc,
                 sa_sc, sb_sc, wa_sc, wb_sc, vprev_sc, trow_sc, erow_sc, qt_sc, *, n_ph):
    et = pl.program_id(1)
    tb = xn_ref.shape[0]
    nk = keys_ref.shape[1]
    phalf = keys_ref.shape[2]
    e_tile = u_ref.shape[0]

    @pl.when(et == 0)
    def _():
        acc_sc[...] = jnp.zeros_like(acc_sc)
        wb_sc[...] = jnp.zeros_like(wb_sc)
        vprev_sc[...] = jnp.zeros_like(vprev_sc)
        xn = xn_ref[...]
        q_rows = wq_ref.shape[0] // 4
        for c in range(4):
            rows = slice(c * q_rows, (c + 1) * q_rows)
            qt_sc[rows, :] = lax.dot_general(wq_ref[rows, :], xn, NT_DIMS,
                                             preferred_element_type=F32).astype(BF16)

        def head(h, carry):
            for c, dst in ((0, s0_sc), (1, s1_sc)):
                hc = 2 * h + c
                q_t = qt_sc[pl.ds(pl.multiple_of(hc * phalf, phalf), phalf), :]
                dst[...] = jnp.dot(keys_ref[hc], q_t, preferred_element_type=F32)
            for lc in range(tb // LANES):
                lanes = slice(lc * LANES, (lc + 1) * LANES)
                rows = [slice(k * SUBLANES, (k + 1) * SUBLANES) for k in range(nk // SUBLANES)]
                s0 = [s0_sc[r, lanes] for r in rows]
                s1 = [s1_sc[r, lanes] for r in rows]
                a = _top_values(s0)
                b = _top_values(s1)
                sel, v16, v17 = _pair_threshold(a, b)
                tau = 0.5 * (v16 + v17)
                z = functools.reduce(jnp.add, [jnp.exp(t - sel[0]) for t in sel])
                rz = 0.5 / z
                tb1 = tau - b[0]
                for r, t0, t1 in zip(rows, s0, s1):
                    tau_sc[h, r, lanes] = jnp.exp(tb1 - t0)
                    e0_sc[h, r, lanes] = jnp.exp(t0 - a[0]) * rz
                    e1_sc[h, r, lanes] = jnp.exp(t1 - b[0])
            return carry

        lax.fori_loop(0, n_ph, head, 0)
        sa_sc[...] = lax.dot_general(u_ref[0:e_tile // 2, :], xn, NT_DIMS, preferred_element_type=F32)

    half = e_tile // 2

    base = pl.multiple_of(et * 2 * SUBLANES, 2 * SUBLANES)
    for h in range(n_ph):
        trow_sc[h] = tau_sc[h, pl.ds(base, 2 * SUBLANES), :]
        erow_sc[h] = e0_sc[h, pl.ds(base, 2 * SUBLANES), :]

    def gates(tile, s_in, w_out, lc, anchor=None):
        rows = slice(tile * SUBLANES, (tile + 1) * SUBLANES)
        lanes = slice(lc * LANES, (lc + 1) * LANES)
        n_jt = nk // SUBLANES
        last = None
        for r in range(SUBLANES):
            g = [None] * n_jt
            for h in range(n_ph):
                t0 = jnp.broadcast_to(trow_sc[h, rows, lanes][r:r + 1], (SUBLANES, LANES))
                ee = jnp.broadcast_to(erow_sc[h, rows, lanes][r:r + 1], (SUBLANES, LANES))
                for jt in range(n_jt):
                    x1 = e1_sc[h, jt * SUBLANES:(jt + 1) * SUBLANES, lanes]
                    term = jnp.where(x1 >= t0, x1, 0.0) * ee
                    g[jt] = term if g[jt] is None else g[jt] + term
            for jp in range(n_jt // 2):
                r0 = r * nk + jp * 2 * SUBLANES
                w_pair = []
                for jj in range(2):
                    sv = s_in[r0 + jj * SUBLANES:r0 + (jj + 1) * SUBLANES, lanes]
                    if anchor is not None:
                        sv, anchor = sv + anchor, None
                    inner = sv * (GELU_C1 + GELU_C2 * (sv * sv))
                    w_tile = g[2 * jp + jj] * (sv + sv * jnp.tanh(inner))
                    w_pair.append(w_tile)
                    last = w_tile if last is None else jnp.maximum(last, w_tile)
                w_out[r0:r0 + 2 * SUBLANES, lanes] = jnp.concatenate(w_pair, axis=0).astype(BF16)
        return last

    d_half = acc_sc.shape[0] // 2
    q_rows = half // 2

    def held(operand, after):
        if after is None:
            return operand
        return operand + _exact_zero(after)[0:1, 0:1].astype(operand.dtype)

    def acc_piece(v_ref, w_ref, p, after):
        rows = slice(p * d_half, (p + 1) * d_half)
        res = acc_sc[rows, :] + jnp.dot(v_ref[rows, :], held(w_ref[...], after),
                                        preferred_element_type=F32)
        acc_sc[rows, :] = res
        return res[-SUBLANES:, -LANES:]

    def score_piece(s_ref, rows_ref, rows, after):
        res = lax.dot_general(rows_ref[rows, :], held(xn_ref[...], after), NT_DIMS,
                              preferred_element_type=F32)
        s_ref[rows, :] = res
        return res[-SUBLANES:, -LANES:]

    ub_ref = u_ref.at[half:, :]
    va_ref = vt_ref.at[:, 0:half]
    lo_rows, hi_rows, all_rows = slice(0, q_rows), slice(q_rows, half), slice(0, half)
    mxu_work = [lambda a: acc_piece(vprev_sc, wb_sc, 0, a), lambda a: acc_piece(vprev_sc, wb_sc, 1, a),
                lambda a: score_piece(sb_sc, ub_ref, lo_rows, a), lambda a: score_piece(sb_sc, ub_ref, hi_rows, a),
                lambda a: acc_piece(va_ref, wa_sc, 0, a), lambda a: acc_piece(va_ref, wa_sc, 1, a),
                lambda a: score_piece(sa_sc, unext_ref, all_rows, a)]
    n_lc = tb // LANES
    n_sec = 2 * n_lc
    first_sec = (0, 0, 0, 1, n_lc, n_lc, n_lc)
    anchored = n_sec > len(mxu_work)
    sec_out, piece_out = [], []
    for sec in range(n_sec):
        tile, lc = divmod(sec, n_lc)
        anchor = None
        for k in range(len(mxu_work)):
            if (k if anchored else first_sec[k]) == sec:
                after = sec_out[sec - 2] if anchored and sec >= 2 else None
                piece_out.append(mxu_work[k](after))
        if anchored and 1 <= sec <= len(mxu_work):
            anchor = _exact_zero(piece_out[sec - 1])
        sec_out.append(gates(tile, (sa_sc, sb_sc)[tile], (wa_sc, wb_sc)[tile], lc, anchor))
    vprev_sc[...] = vt_ref[:, half:]

    @pl.when(et == pl.num_programs(1) - 1)
    def _():
        tail = jnp.dot(vprev_sc[...], wb_sc[...], preferred_element_type=F32)
        out_ref[...] = x_ref[...] + (acc_sc[...] + tail).T


def _peer(xn2, x2d, wq_t, keys, u_b, v_t):
    t, d = x2d.shape
    n_hc, nk, phalf = keys.shape
    n_ph = n_hc // 2
    n_exp = u_b.shape[0]
    tb = _pick(t, (512, 256))
    half = SUBLANES * nk
    e_tile = 2 * half
    assert n_exp % e_tile == 0 and tb % MXU_COLS == 0
    n_steps = n_exp // e_tile
    kern = functools.partial(_peer_kernel, n_ph=n_ph)
    head_buf = lambda: pltpu.VMEM((n_ph, nk, tb), F32)
    once = dict(pipeline_mode=pl.Buffered(1))
    return pl.pallas_call(
        kern,
        grid=(t // tb, n_exp // e_tile),
        in_specs=[pl.BlockSpec((tb, d), lambda i, e: (i, 0), **once),
                  pl.BlockSpec((tb, d), lambda i, e: (i, 0), **once),
                  pl.BlockSpec(wq_t.shape, lambda i, e: (0, 0), **once),
                  pl.BlockSpec(keys.shape, lambda i, e: (0, 0, 0), **once),
                  pl.BlockSpec((e_tile, d), lambda i, e: (e, 0)),
                  pl.BlockSpec((half, d), lambda i, e: (jnp.minimum(2 * e + 2, 2 * n_steps - 2), 0)),
                  pl.BlockSpec((d, e_tile), lambda i, e: (0, e))],
        out_specs=pl.BlockSpec((tb, d), lambda i, e: (i, 0)),
        out_shape=jax.ShapeDtypeStruct((t, d), F32),
        scratch_shapes=[head_buf(), head_buf(), pltpu.VMEM((nk, tb), F32), head_buf(),
                        pltpu.VMEM((nk, tb), F32),
                        pltpu.VMEM((d, tb), F32),
                        pltpu.VMEM((half, tb), F32), pltpu.VMEM((half, tb), F32),
                        pltpu.VMEM((half, tb), BF16), pltpu.VMEM((half, tb), BF16),
                        pltpu.VMEM((d, half), BF16),
                        pltpu.VMEM((n_ph, 2 * SUBLANES, tb), F32), pltpu.VMEM((n_ph, 2 * SUBLANES, tb), F32),
                        pltpu.VMEM((wq_t.shape[0], tb), BF16)],
        compiler_params=_params(("parallel", "arbitrary")),
        name="peer",
    )(xn2, x2d, wq_t, keys, u_b, u_b, v_t)


def _norm_kernel(x_ref, w_ref, o_ref):
    o_ref[...] = _rms(x_ref[...], w_ref[...])


def _final_norm(x2d, w):
    t, d = x2d.shape
    tm = _pick(t, (1024, 512, 256, 128))
    return pl.pallas_call(
        _norm_kernel,
        grid=(t // tm,),
        in_specs=[pl.BlockSpec((tm, d), lambda i: (i, 0)), pl.BlockSpec((1, d), lambda i: (0, 0))],
        out_specs=pl.BlockSpec((tm, d), lambda i: (i, 0)),
        out_shape=jax.ShapeDtypeStruct((t, d), F32),
        compiler_params=_params(("parallel",)),
        name="final_norm",
    )(x2d, w.reshape(1, d))


def _prep_layer(l, w, dims):
    m_dim, n_mh, s_dim, cd, n_sh, d = dims
    w_in = w["w_in"][l]
    o = [0]
    def take(n):
        o[0] += n
        return w_in[:, o[0] - n:o[0]]
    qkvo, w_ig, w_fg = take(4 * m_dim), take(n_mh), take(n_mh)
    w_z, w_xbc, w_dt, w_g = take(s_dim), take(cd), take(n_sh), take(2 * d)
    n_small = 2 * n_mh + n_sh
    w_small = jnp.concatenate([w_ig, w_fg, w_dt, jnp.zeros((d, SMALL_W - n_small), F32)], axis=1)
    b_small = jnp.concatenate([w["b_igate"][l], w["b_fgate"][l], w["dt_bias"][l],
                               jnp.zeros((SMALL_W - n_small,), F32)]).reshape(1, SMALL_W)
    keys = w["peer_keys"][l]
    return dict(
        norm1=w["norm1_w"][l].reshape(1, d),
        w_big=jnp.concatenate([qkvo, w_xbc, w_z, w_g], axis=1).astype(BF16),
        w_small=w_small.astype(BF16), b_small=b_small,
        w_a=w["w_a"][l].astype(BF16), w_b=w["w_b"][l].astype(BF16), w_out=w["w_out"][l].astype(BF16),
        norm2=w["norm2_w"][l].reshape(1, d),
        wq_t=w["peer_wq"][l].T.astype(BF16),
        keys=keys.reshape((keys.shape[0] * 2,) + keys.shape[2:]).astype(BF16),
        u_b=w["peer_u"][l].astype(BF16), v_t=w["peer_v"][l].T.astype(BF16),
    )


def _trunk(x, c0, n0, m0, ssm0, conv0, w, prepped, dims):
    m_dim, n_mh, s_dim, cd, n_sh, d = dims
    b, s, _ = x.shape
    x2 = x.reshape(b * s, d)
    outs = [[] for _ in range(5)]
    for l, p in enumerate(prepped):
        big2, small2 = _in_proj(x2, p["norm1"], p["w_big"], p["w_small"], p["b_small"])
        big3 = big2.reshape(b, s, -1)
        small3 = small2.reshape(b, s, SMALL_W)
        small_t = small3.transpose(0, 2, 1)
        hm, c, n, m = _mlstm(big3, small_t, c0[l], n0[l], m0[l], w["mlstm_norm_w"][l])
        ys, ssm, conv = _ssd(big3, small3, small_t, conv0[l], ssm0[l], w["conv_w"][l], w["conv_b"][l],
                             w["a_log"][l], w["d_skip"][l], w["ssm_norm_w"][l], m_dim=m_dim, n_mh=n_mh)
        ga_blk = (4 * m_dim + cd + s_dim) // d
        x1, xn2 = _out_proj(x2, hm.reshape(b * s, -1), ys.reshape(b * s, -1), big2,
                            p["w_a"], p["w_b"], p["w_out"], p["norm2"], ga_blk)
        x2 = _peer(xn2, x1, p["wq_t"], p["keys"], p["u_b"], p["v_t"])
        for lst, val in zip(outs, (c, n.reshape(b, n_mh, -1), m.reshape(b, n_mh), ssm, conv)):
            lst.append(val)
    y = _final_norm(x2, w["final_norm_w"]).reshape(b, s, d)
    return (y,) + tuple(jnp.stack(o) for o in outs)


def kernel(x_prompt, x_sample, state_mlstm_C, state_mlstm_n, state_mlstm_m, state_ssm, cache_conv,
           norm1_w, w_in, b_igate, b_fgate, mlstm_norm_w, conv_w, conv_b, dt_bias, a_log, d_skip,
           ssm_norm_w, w_a, w_b, w_out, norm2_w, peer_wq, peer_keys, peer_u, peer_v, final_norm_w):
    w = dict(norm1_w=norm1_w, w_in=w_in, b_igate=b_igate, b_fgate=b_fgate, mlstm_norm_w=mlstm_norm_w,
             conv_w=conv_w, conv_b=conv_b, dt_bias=dt_bias, a_log=a_log, d_skip=d_skip,
             ssm_norm_w=ssm_norm_w, w_a=w_a, w_b=w_b, w_out=w_out, norm2_w=norm2_w, peer_wq=peer_wq,
             peer_keys=peer_keys, peer_u=peer_u, peer_v=peer_v, final_norm_w=final_norm_w)
    depth, _, n_mh, dh, _ = state_mlstm_C.shape
    n_sh, pdim = state_ssm.shape[2], state_ssm.shape[3]
    d = x_prompt.shape[-1]
    dims = (n_mh * dh, n_mh, n_sh * pdim, cache_conv.shape[-1], n_sh, d)
    prepped = [_prep_layer(l, w, dims) for l in range(depth)]
    bp = x_prompt.shape[0]
    zeros = lambda a: jnp.zeros((depth, bp) + a.shape[2:], F32)
    yp = _trunk(x_prompt, zeros(state_mlstm_C), zeros(state_mlstm_n), zeros(state_mlstm_m),
                zeros(state_ssm), zeros(cache_conv), w, prepped, dims)
    ys = _trunk(x_sample, state_mlstm_C, state_mlstm_n, state_mlstm_m, state_ssm, cache_conv,
                w, prepped, dims)
    return (yp[0], ys[0]) + yp[1:] + ys[1:]
```

```python
import functools

import jax
import jax.numpy as jnp
from jax import lax
from jax.experimental import pallas as pl
from jax.experimental.pallas import tpu as pltpu

F32 = jnp.float32
BF16 = jnp.bfloat16
EPS = 1e-6
CONV_W = 4
PEER_TOPK = 16
LANES = 128
SUBLANES = 8
MXU_COLS = 256
SMALL_W = 128
VMEM_LIMIT = 56 * 1024 * 1024
NT_DIMS = (((1,), (1,)), ((), ()))


def _pick(n, prefs):
    for p in prefs:
        if n % p == 0:
            return p
    return n


def _softplus(x):
    return jnp.maximum(x, 0.0) + jnp.log1p(jnp.exp(-jnp.abs(x)))


def _log_sigmoid(x):
    return jnp.minimum(x, 0.0) - jnp.log1p(jnp.exp(-jnp.abs(x)))


def _sigmoid(x):
    return 0.5 * jnp.tanh(0.5 * x) + 0.5


def _silu(x):
    h = 0.5 * x
    return h + h * jnp.tanh(h)


def _split3(x):
    hi = x.astype(BF16)
    r1 = x - hi.astype(F32)
    mid = r1.astype(BF16)
    lo = (r1 - mid.astype(F32)).astype(BF16)
    return hi, mid, lo


def _exact_zero(tile):
    bits = pltpu.bitcast(tile, jnp.uint32)
    sixteen = jnp.uint32(16)
    return pltpu.bitcast(lax.shift_right_logical(lax.shift_right_logical(bits, sixteen), sixteen), F32)


def _rms(x, w):
    return x * lax.rsqrt(jnp.mean(x * x, axis=-1, keepdims=True) + EPS) * w


def _params(sem):
    return pltpu.CompilerParams(dimension_semantics=sem, vmem_limit_bytes=VMEM_LIMIT)


def _in_proj_kernel(x_ref, nw_ref, wbig_ref, wsmall_ref, bsmall_ref, big_ref, small_ref, h_sc):
    @pl.when(pl.program_id(1) == 0)
    def _():
        hb = _rms(x_ref[...], nw_ref[...]).astype(BF16)
        h_sc[...] = hb
        small_ref[...] = jnp.dot(hb, wsmall_ref[...], preferred_element_type=F32) + bsmall_ref[...]

    big_ref[...] = jnp.dot(h_sc[...], wbig_ref[...], preferred_element_type=F32).astype(BF16)


def _in_proj(x2d, nw, w_big, w_small, b_small):
    t, d = x2d.shape
    nbig = w_big.shape[1]
    tm = _pick(t, (1024, 512, 256, 128))
    tn = _pick(nbig, (2048, 1536, 1024, 512, 256, 128))
    return pl.pallas_call(
        _in_proj_kernel,
        grid=(t // tm, nbig // tn),
        in_specs=[
            pl.BlockSpec((tm, d), lambda i, j: (i, 0)),
            pl.BlockSpec((1, d), lambda i, j: (0, 0)),
            pl.BlockSpec((d, tn), lambda i, j: (0, j)),
            pl.BlockSpec((d, SMALL_W), lambda i, j: (0, 0)),
            pl.BlockSpec((1, SMALL_W), lambda i, j: (0, 0)),
        ],
        out_specs=[
            pl.BlockSpec((tm, tn), lambda i, j: (i, j)),
            pl.BlockSpec((tm, SMALL_W), lambda i, j: (i, 0)),
        ],
        out_shape=[jax.ShapeDtypeStruct((t, nbig), BF16), jax.ShapeDtypeStruct((t, SMALL_W), F32)],
        scratch_shapes=[pltpu.VMEM((tm, d), BF16)],
        compiler_params=_params(("parallel", "arbitrary")),
        name="in_proj",
    )(x2d, nw, w_big, w_small, b_small)


def _mlstm_kernel(q_ref, k_ref, v_ref, og_ref, g_ref, c0_ref, n0_ref, m0_ref, nw_ref,
                  h_ref, c_ref, n_ref, m_ref, c_sc, n_sc, m_sc, *, n_heads, scale):
    ci = pl.program_id(1)
    lc = q_ref.shape[1]
    dh = c_sc.shape[2]

    @pl.when(ci == 0)
    def _():
        c_sc[...] = c0_ref[...]
        n_sc[...] = n0_ref[...]
        m_sc[...] = m0_ref[...]

    row = lax.broadcasted_iota(jnp.int32, (lc, lc), 0)
    col = lax.broadcasted_iota(jnp.int32, (lc, lc), 1)
    causal = col <= row
    tri = causal.astype(BF16)
    triu = (row <= col).astype(BF16)
    for bb, hh in [(bb, hh) for bb in range(q_ref.shape[0]) for hh in range(n_heads)]:
        gates = g_ref[bb]
        cols = slice(hh * dh, (hh + 1) * dh)
        ig = gates[hh:hh + 1, :]
        lf = _log_sigmoid(gates[n_heads + hh:n_heads + hh + 1, :])
        lf_parts = _split3(jnp.broadcast_to(lf, (SUBLANES, lc)))
        fcum_col = sum(lax.dot_general(tri, p, NT_DIMS, preferred_element_type=F32)
                       for p in lf_parts)[:, 0:1]
        fcum_row = sum(jnp.dot(p, triu, preferred_element_type=F32) for p in lf_parts)[0:1, :]

        m_prev = m_sc[bb, hh]
        logw = jnp.where(causal, fcum_col - fcum_row + ig, -jnp.inf)
        log_prev = fcum_col + m_prev
        m_t = jnp.maximum(log_prev, jnp.max(logw, axis=1, keepdims=True))
        a_prev = jnp.exp(log_prev - m_t)
        decay = jnp.exp(logw - m_t)

        q = q_ref[bb, :, cols]
        k = k_ref[bb, :, cols]
        v = v_ref[bb, :, cols]
        s = lax.dot_general(q, k, NT_DIMS, preferred_element_type=F32) * scale * decay
        num = (jnp.dot(s.astype(BF16), v, preferred_element_type=F32)
               + a_prev * jnp.dot(q, c_sc[bb, hh].astype(BF16), preferred_element_type=F32))
        qn = jnp.sum(q.astype(F32) * n_sc[bb, hh], axis=1, keepdims=True)
        den = jnp.sum(s, axis=1, keepdims=True) + a_prev * qn
        hout = num / jnp.maximum(jnp.abs(den), jnp.exp(-m_t))
        y = _rms(hout, nw_ref[hh]) * _sigmoid(og_ref[bb, :, cols].astype(F32))
        h_ref[bb, :, cols] = y.astype(BF16)

        f_end = fcum_row[:, lc - 1:lc]
        m_new = m_t[lc - 1:lc, :]
        w_row = jnp.exp(f_end - fcum_row + ig - m_new) * scale
        a_end = jnp.exp(f_end + m_prev - m_new)
        kw = (k.astype(F32).T * w_row).astype(BF16)
        c_sc[bb, hh] = a_end * c_sc[bb, hh] + jnp.dot(kw, v, preferred_element_type=F32)
        w8 = jnp.broadcast_to(w_row, (SUBLANES, lc)).astype(BF16)
        n_sc[bb, hh] = a_end * n_sc[bb, hh] + jnp.dot(w8, k, preferred_element_type=F32)[0:1, :]
        m_sc[bb, hh] = m_new

    @pl.when(ci == pl.num_programs(1) - 1)
    def _():
        c_ref[...] = c_sc[...]
        n_ref[...] = n_sc[...]
        m_ref[...] = m_sc[...]


def _mlstm(big3, small_t, c0, n0, m0, norm_w):
    b, s, _ = big3.shape
    _, nh, dh, _ = c0.shape
    m_dim = nh * dh
    lc = _pick(s, (256, 128, 64, 32, 16))
    bg = _pick(b, (2, 1))
    kern = functools.partial(_mlstm_kernel, n_heads=nh, scale=float(dh) ** -0.5)
    blk = lambda off: pl.BlockSpec((bg, lc, m_dim), lambda bi, ci: (bi, ci, off))
    st4 = lambda r, c: pl.BlockSpec((bg, nh, r, c), lambda bi, ci: (bi, 0, 0, 0))
    return pl.pallas_call(
        kern,
        grid=(b // bg, s // lc),
        in_specs=[blk(0), blk(1), blk(2), blk(3),
                  pl.BlockSpec((bg, SUBLANES, lc), lambda bi, ci: (bi, 0, ci)),
                  st4(dh, dh), st4(1, dh), st4(1, 1),
                  pl.BlockSpec((nh, 1, dh), lambda bi, ci: (0, 0, 0))],
        out_specs=[pl.BlockSpec((bg, lc, m_dim), lambda bi, ci: (bi, ci, 0)),
                   st4(dh, dh), st4(1, dh), st4(1, 1)],
        out_shape=[jax.ShapeDtypeStruct((b, s, m_dim), BF16),
                   jax.ShapeDtypeStruct((b, nh, dh, dh), F32),
                   jax.ShapeDtypeStruct((b, nh, 1, dh), F32),
                   jax.ShapeDtypeStruct((b, nh, 1, 1), F32)],
        scratch_shapes=[pltpu.VMEM((bg, nh, dh, dh), F32), pltpu.VMEM((bg, nh, 1, dh), F32),
                        pltpu.VMEM((bg, nh, 1, 1), F32)],
        compiler_params=_params(("parallel", "arbitrary")),
        name="mlstm",
    )(big3, big3, big3, big3, small_t, c0, n0.reshape(b, nh, 1, dh), m0.reshape(b, nh, 1, 1),
      norm_w.reshape(nh, 1, dh))


def _ssd_kernel(xbc_ref, z_ref, sm_ref, smt_ref, conv0_ref, ssm0_ref, cw_ref, cb_ref,
                alr_ref, alc_ref, dsk_ref, nw_ref, ex_ref,
                ys_ref, ssm_ref, conv_ref, xpad, ht_sc,
                *, dt_off, n_sh, n_groups, n_state, s_dim):
    ci = pl.program_id(1)
    lc = xbc_ref.shape[1]
    gw = s_dim // n_groups
    hpg = n_sh // n_groups
    pdim = gw // hpg
    pad = SUBLANES

    @pl.when(ci == 0)
    def _():
        xpad[pad - (CONV_W - 1):pad, :] = conv0_ref[0]
        for g in range(n_groups):
            ht_sc[g] = ssm0_ref[0, g].T

    xpad[pad:pad + lc, :] = xbc_ref[0].astype(F32)
    conv = cb_ref[...]
    for i in range(CONV_W):
        conv = conv + xpad[pad - (CONV_W - 1) + i:pad - (CONV_W - 1) + i + lc, :] * cw_ref[i:i + 1, :]
    tail = xpad[pad + lc - (CONV_W - 1):pad + lc, :]
    xpad[pad - (CONV_W - 1):pad, :] = tail
    act = _silu(conv)
    xs = act[:, :s_dim]
    bm = act[:, s_dim:s_dim + n_groups * n_state]
    cm = act[:, s_dim + n_groups * n_state:]

    dt_col = _softplus(sm_ref[0][:, dt_off:dt_off + n_sh])
    dt_row = _softplus(smt_ref[0][dt_off:dt_off + n_sh, :])
    la_col = dt_col * (-jnp.exp(alr_ref[...]))
    la_row = dt_row * (-jnp.exp(alc_ref[...]))
    row = lax.broadcasted_iota(jnp.int32, (lc, lc), 0)
    col = lax.broadcasted_iota(jnp.int32, (lc, lc), 1)
    causal = col <= row
    tri = causal.astype(BF16)
    triu = (row <= col).astype(BF16)
    cum_col = sum(jnp.dot(tri, p, preferred_element_type=F32) for p in _split3(la_col))
    cum_row = sum(jnp.dot(p, triu, preferred_element_type=F32) for p in _split3(la_row))
    w_end = jnp.exp(cum_col[lc - 1:lc, :] - cum_col) * dt_col
    ex3 = ex_ref[...]
    ecum_x = jnp.dot(jnp.concatenate(_split3(jnp.exp(cum_col)), axis=1), ex3, preferred_element_type=F32)
    wend_x = jnp.dot(jnp.concatenate(_split3(w_end), axis=1), ex3, preferred_element_type=F32)
    xw = (xs * wend_x).astype(BF16)
    xs_b = xs.astype(BF16)
    head_of_lane = lax.div(lax.broadcasted_iota(jnp.int32, (1, gw), 1), pdim)

    parts = []
    for g in range(n_groups):
        bm_g = bm[:, g * n_state:(g + 1) * n_state]
        cm_b = cm[:, g * n_state:(g + 1) * n_state].astype(BF16)
        cb = lax.dot_general(cm_b, bm_g.astype(BF16), NT_DIMS, preferred_element_type=F32)
        xg = xs_b[:, g * gw:(g + 1) * gw]
        yg = (jnp.dot(cm_b, ht_sc[g].astype(BF16), preferred_element_type=F32)
              * ecum_x[:, g * gw:(g + 1) * gw])
        for kk in range(hpg):
            hd = g * hpg + kk
            dec = jnp.exp(jnp.where(causal, cum_col[:, hd:hd + 1] - cum_row[hd:hd + 1, :], -jnp.inf))
            wm = (cb * dec * dt_row[hd:hd + 1, :]).astype(BF16)
            xm = jnp.where(head_of_lane == kk, xg, jnp.zeros_like(xg))
            yg = yg + jnp.dot(wm, xm, preferred_element_type=F32)
        ht_sc[g] = (ecum_x[lc - 1:lc, g * gw:(g + 1) * gw] * ht_sc[g]
                    + jnp.dot(bm_g.T.astype(BF16), xw[:, g * gw:(g + 1) * gw], preferred_element_type=F32))
        parts.append(yg)
    y = jnp.concatenate(parts, axis=1) + dsk_ref[...] * xs
    zf = z_ref[0].astype(F32)
    y = y * _silu(zf)
    ys_ref[0] = _rms(y, nw_ref[...]).astype(BF16)

    @pl.when(ci == pl.num_programs(1) - 1)
    def _():
        conv_ref[0] = tail
        for g in range(n_groups):
            ssm_ref[0, g] = ht_sc[g].T


def _ssd(big3, small3, small_t, conv0, ssm0, conv_w, conv_b, a_log, d_skip, norm_w, *, m_dim, n_mh):
    b, s, _ = big3.shape
    _, n_sh, pdim, n_state = ssm0.shape
    cd = conv0.shape[2]
    s_dim = n_sh * pdim
    n_groups = (cd - s_dim) // (2 * n_state)
    gw = s_dim // n_groups
    lc = _pick(s, (256, 128, 64, 32, 16))
    xbc_blk = (4 * m_dim) // cd
    z_blk = (4 * m_dim + cd) // s_dim
    assert xbc_blk * cd == 4 * m_dim and z_blk * s_dim == 4 * m_dim + cd
    kern = functools.partial(_ssd_kernel, dt_off=2 * n_mh, n_sh=n_sh, n_groups=n_groups,
                             n_state=n_state, s_dim=s_dim)
    full2 = lambda r, c: pl.BlockSpec((r, c), lambda bi, ci: (0, 0))
    expander = jnp.tile(jnp.repeat(jnp.eye(n_sh, dtype=BF16), pdim, axis=1), (3, 1))
    ys, ssm, conv = pl.pallas_call(
        kern,
        grid=(b, s // lc),
        in_specs=[pl.BlockSpec((1, lc, cd), lambda bi, ci: (bi, ci, xbc_blk)),
                  pl.BlockSpec((1, lc, s_dim), lambda bi, ci: (bi, ci, z_blk)),
                  pl.BlockSpec((1, lc, SMALL_W), lambda bi, ci: (bi, ci, 0)),
                  pl.BlockSpec((1, SMALL_W, lc), lambda bi, ci: (bi, 0, ci)),
                  pl.BlockSpec((1, CONV_W - 1, cd), lambda bi, ci: (bi, 0, 0)),
                  pl.BlockSpec((1, n_groups, gw, n_state), lambda bi, ci: (bi, 0, 0, 0)),
                  full2(CONV_W, cd), full2(1, cd), full2(1, n_sh), full2(n_sh, 1),
                  full2(1, s_dim), full2(1, s_dim), full2(3 * n_sh, s_dim)],
        out_specs=[pl.BlockSpec((1, lc, s_dim), lambda bi, ci: (bi, ci, 0)),
                   pl.BlockSpec((1, n_groups, gw, n_state), lambda bi, ci: (bi, 0, 0, 0)),
                   pl.BlockSpec((1, CONV_W - 1, cd), lambda bi, ci: (bi, 0, 0))],
        out_shape=[jax.ShapeDtypeStruct((b, s, s_dim), BF16),
                   jax.ShapeDtypeStruct((b, n_groups, gw, n_state), F32),
                   jax.ShapeDtypeStruct((b, CONV_W - 1, cd), F32)],
        scratch_shapes=[pltpu.VMEM((lc + SUBLANES, cd), F32), pltpu.VMEM((n_groups, n_state, gw), F32)],
        compiler_params=_params(("parallel", "arbitrary")),
        name="ssd",
    )(big3, big3, small3, small_t, conv0, ssm0.reshape(b, n_groups, gw, n_state), conv_w,
      conv_b.reshape(1, cd), a_log.reshape(1, n_sh), a_log.reshape(n_sh, 1),
      jnp.repeat(d_skip, pdim).reshape(1, s_dim), norm_w.reshape(1, s_dim), expander)
    return ys, ssm.reshape(b, n_sh, pdim, n_state), conv


def _out_kernel(x_ref, hm_ref, ys_ref, ga_ref, gb_ref, wa_ref, wb_ref, wo_ref, n2_ref, xo_ref, xn_ref):
    a = jnp.dot(hm_ref[...], wa_ref[...], preferred_element_type=F32)
    bb = jnp.dot(ys_ref[...], wb_ref[...], preferred_element_type=F32)
    mix = _sigmoid(ga_ref[...].astype(F32)) * a + _sigmoid(gb_ref[...].astype(F32)) * bb
    xo = x_ref[...] + jnp.dot(mix.astype(BF16), wo_ref[...], preferred_element_type=F32)
    xo_ref[...] = xo
    xn_ref[...] = _rms(xo, n2_ref[...]).astype(BF16)


def _out_proj(x2d, hm2, ys2, big2, w_a, w_b, w_out, norm2_w, ga_blk):
    t, d = x2d.shape
    tm = _pick(t, (512, 256, 128))
    full = lambda a: pl.BlockSpec(a.shape, lambda i: (0, 0))
    rows = lambda w: pl.BlockSpec((tm, w), lambda i: (i, 0))
    return pl.pallas_call(
        _out_kernel,
        grid=(t // tm,),
        in_specs=[rows(d), rows(hm2.shape[1]), rows(ys2.shape[1]),
                  pl.BlockSpec((tm, d), lambda i: (i, ga_blk)),
                  pl.BlockSpec((tm, d), lambda i: (i, ga_blk + 1)),
                  full(w_a), full(w_b), full(w_out), full(norm2_w)],
        out_specs=[rows(d), rows(d)],
        out_shape=[jax.ShapeDtypeStruct((t, d), F32), jax.ShapeDtypeStruct((t, d), BF16)],
        compiler_params=_params(("parallel",)),
        name="out_proj",
    )(x2d, hm2, ys2, big2, big2, w_a, w_b, w_out, norm2_w)


N_TOP = PEER_TOPK + 1
GELU_C1 = (2.0 / 3.141592653589793) ** 0.5
GELU_C2 = GELU_C1 * 0.044715


def _sort_network(n):
    def merge(lo, hi, r):
        step = 2 * r
        if step < hi - lo:
            yield from merge(lo, hi, step)
            yield from merge(lo + r, hi, step)
            yield from ((i, i + r) for i in range(lo + r, hi - r, step))
        else:
            yield (lo, lo + r)

    def sort(lo, hi):
        if hi > lo:
            mid = lo + (hi - lo) // 2
            yield from sort(lo, mid)
            yield from sort(mid + 1, hi)
            yield from merge(lo, hi, 1)

    return tuple(sort(0, n - 1))


def _all_sublanes(op, x):
    shift = SUBLANES // 2
    while shift:
        x = op(x, pltpu.roll(x, shift, 0))
        shift //= 2
    return x


def _merge_columns(v):
    n = len(v)
    dropped = None
    shift = SUBLANES // 2
    while shift:
        partner = [pltpu.roll(v[n - 1 - k], shift, 0) for k in range(n)]
        lo = functools.reduce(jnp.maximum, [jnp.minimum(v[k], partner[k]) for k in range(n)])
        dropped = lo if dropped is None else jnp.maximum(dropped, lo)
        v = [jnp.maximum(v[k], partner[k]) for k in range(n)]
        stride = n // 2
        while stride:
            for i in range(n):
                if not i & stride:
                    v[i], v[i + stride] = jnp.maximum(v[i], v[i + stride]), jnp.minimum(v[i], v[i + stride])
            stride //= 2
        shift //= 2
    return v, dropped


def _top_values(tiles):
    v = list(tiles)
    assert len(v) == PEER_TOPK
    for i, j in _sort_network(len(v)):
        v[i], v[j] = jnp.maximum(v[i], v[j]), jnp.minimum(v[i], v[j])
    top, dropped = _merge_columns(v)
    return top + [_all_sublanes(jnp.maximum, dropped)]


def _pair_threshold(a, b):
    sub = lax.broadcasted_iota(jnp.int32, (SUBLANES, LANES), 0)
    b0 = b[SUBLANES - 1]
    for s in reversed(range(SUBLANES - 1)):
        b0 = jnp.where(sub == s, b[s], b0)
    top, dropped = _merge_columns([b0 + a[k] for k in range(PEER_TOPK)])
    t17 = _all_sublanes(jnp.maximum, dropped)
    x = a[PEER_TOPK] + b[0]
    extra = []
    for l in range(SUBLANES, N_TOP):
        y = a[0] + b[l]
        extra.append(jnp.maximum(y, x))
        x = jnp.minimum(y, x)
    extra.append(x)
    sel = list(top)
    for j, e in enumerate(extra):
        k = PEER_TOPK - 1 - j
        t17 = jnp.maximum(t17, jnp.minimum(top[k], e))
        sel[k] = jnp.maximum(top[k], e)
    v16 = functools.reduce(jnp.minimum, sel[PEER_TOPK - 1 - len(extra):])
    return sel, v16, t17


def _peer_kernel(xn_ref, x_ref, wq_ref, keys_ref, u_ref, unext_ref, vt_ref, out_ref,
                 tau_sc, e0_sc, s1_sc, e1_sc, s0_sc, acc_sc,
                 sa_sc, sb_sc, wa_sc, wb_sc, vprev_sc, trow_sc, erow_sc, qt_sc, *, n_ph):
    et = pl.program_id(1)
    tb = xn_ref.shape[0]
    nk = keys_ref.shape[1]
    phalf = keys_ref.shape[2]
    e_tile = u_ref.shape[0]

    @pl.when(et == 0)
    def _():
        acc_sc[...] = jnp.zeros_like(acc_sc)
        wb_sc[...] = jnp.zeros_like(wb_sc)
        vprev_sc[...] = jnp.zeros_like(vprev_sc)
        xn = xn_ref[...]
        q_rows = wq_ref.shape[0] // 4
        for c in range(4):
            rows = slice(c * q_rows, (c + 1) * q_rows)
            qt_sc[rows, :] = lax.dot_general(wq_ref[rows, :], xn, NT_DIMS,
                                             preferred_element_type=F32).astype(BF16)

        def head(h, carry):
            for c, dst in ((0, s0_sc), (1, s1_sc)):
                hc = 2 * h + c
                q_t = qt_sc[pl.ds(pl.multiple_of(hc * phalf, phalf), phalf), :]
                dst[...] = jnp.dot(keys_ref[hc], q_t, preferred_element_type=F32)
            for lc in range(tb // LANES):
                lanes = slice(lc * LANES, (lc + 1) * LANES)
                rows = [slice(k * SUBLANES, (k + 1) * SUBLANES) for k in range(nk // SUBLANES)]
                s0 = [s0_sc[r, lanes] for r in rows]
                s1 = [s1_sc[r, lanes] for r in rows]
                a = _top_values(s0)
                b = _top_values(s1)
                sel, v16, v17 = _pair_threshold(a, b)
                tau = 0.5 * (v16 + v17)
                z = functools.reduce(jnp.add, [jnp.exp(t - sel[0]) for t in sel])
                rz = 0.5 / z
                tb1 = tau - b[0]
                for r, t0, t1 in zip(rows, s0, s1):
                    tau_sc[h, r, lanes] = jnp.exp(tb1 - t0)
                    e0_sc[h, r, lanes] = jnp.exp(t0 - a[0]) * rz
                    e1_sc[h, r, lanes] = jnp.exp(t1 - b[0])
            return carry

        lax.fori_loop(0, n_ph, head, 0)
        sa_sc[...] = lax.dot_general(u_ref[0:e_tile // 2, :], xn, NT_DIMS, preferred_element_type=F32)

    half = e_tile // 2

    base = pl.multiple_of(et * 2 * SUBLANES, 2 * SUBLANES)
    for h in range(n_ph):
        trow_sc[h] = tau_sc[h, pl.ds(base, 2 * SUBLANES), :]
        erow_sc[h] = e0_sc[h, pl.ds(base, 2 * SUBLANES), :]

    def gates(tile, s_in, w_out, lc, anchor=None):
        rows = slice(tile * SUBLANES, (tile + 1) * SUBLANES)
        lanes = slice(lc * LANES, (lc + 1) * LANES)
        n_jt = nk // SUBLANES
        last = None
        for r in range(SUBLANES):
            g = [None] * n_jt
            for h in range(n_ph):
                t0 = jnp.broadcast_to(trow_sc[h, rows, lanes][r:r + 1], (SUBLANES, LANES))
                ee = jnp.broadcast_to(erow_sc[h, rows, lanes][r:r + 1], (SUBLANES, LANES))
                for jt in range(n_jt):
                    x1 = e1_sc[h, jt * SUBLANES:(jt + 1) * SUBLANES, lanes]
                    term = jnp.where(x1 >= t0, x1, 0.0) * ee
                    g[jt] = term if g[jt] is None else g[jt] + term
            for jp in range(n_jt // 2):
                r0 = r * nk + jp * 2 * SUBLANES
                w_pair = []
                for jj in range(2):
                    sv = s_in[r0 + jj * SUBLANES:r0 + (jj + 1) * SUBLANES, lanes]
                    if anchor is not None:
                        sv, anchor = sv + anchor, None
                    inner = sv * (GELU_C1 + GELU_C2 * (sv * sv))
                    w_tile = g[2 * jp + jj] * (sv + sv * jnp.tanh(inner))
                    w_pair.append(w_tile)
                    last = w_tile if last is None else jnp.maximum(last, w_tile)
                w_out[r0:r0 + 2 * SUBLANES, lanes] = jnp.concatenate(w_pair, axis=0).astype(BF16)
        return last

    def held(operand, after):
        if after is None:
            return operand
        return operand + _exact_zero(after)[0:1, 0:1].astype(operand.dtype)

    def acc_piece(v_ref, w_ref, after):
        res = acc_sc[...] + jnp.dot(v_ref[...], held(w_ref[...], after), preferred_element_type=F32)
        acc_sc[...] = res
        return res[-SUBLANES:, -LANES:]

    def score_piece(s_ref, rows_ref, after):
        res = lax.dot_general(rows_ref[...], held(xn_ref[...], after), NT_DIMS,
                              preferred_element_type=F32)
        s_ref[...] = res
        return res[-SUBLANES:, -LANES:]

    ub_ref = u_ref.at[half:, :]
    va_ref = vt_ref.at[:, 0:half]
    n_lc = tb // LANES
    n_sec = 2 * n_lc
    mxu_work = [(lambda a: acc_piece(vprev_sc, wb_sc, a), 0, n_lc // 2),
                (lambda a: score_piece(sb_sc, ub_ref, a), min(1, n_lc - 1), n_lc),
                (lambda a: acc_piece(va_ref, wa_sc, a), n_lc, n_lc + n_lc // 2),
                (lambda a: score_piece(sa_sc, unext_ref, a), n_lc + 1, n_sec - 1)]
    anchored = n_lc >= 4
    sec_out, due = [], {}
    for sec in range(n_sec):
        tile, lc = divmod(sec, n_lc)
        for fn, at, by in mxu_work:
            if at == sec:
                after = sec_out[sec - 2] if anchored and sec >= 2 else None
                due[by] = fn(after)
        anchor = _exact_zero(due[sec]) if anchored and sec in due and sec > 0 else None
        sec_out.append(gates(tile, (sa_sc, sb_sc)[tile], (wa_sc, wb_sc)[tile], lc, anchor))
    vprev_sc[...] = vt_ref[:, half:]

    @pl.when(et == pl.num_programs(1) - 1)
    def _():
        tail = jnp.dot(vprev_sc[...], wb_sc[...], preferred_element_type=F32)
        out_ref[...] = x_ref[...] + (acc_sc[...] + tail).T


def _peer(xn2, x2d, wq_t, keys, u_b, v_t):
    t, d = x2d.shape
    n_hc, nk, phalf = keys.shape
    n_ph = n_hc // 2
    n_exp = u_b.shape[0]
    tb = _pick(t, (512, 256))
    half = SUBLANES * nk
    e_tile = 2 * half
    assert n_exp % e_tile == 0 and tb % MXU_COLS == 0
    n_steps = n_exp // e_tile
    kern = functools.partial(_peer_kernel, n_ph=n_ph)
    head_buf = lambda: pltpu.VMEM((n_ph, nk, tb), F32)
    once = dict(pipeline_mode=pl.Buffered(1))
    return pl.pallas_call(
        kern,
        grid=(t // tb, n_exp // e_tile),
        in_specs=[pl.BlockSpec((tb, d), lambda i, e: (i, 0), **once),
                  pl.BlockSpec((tb, d), lambda i, e: (i, 0), **once),
                  pl.BlockSpec(wq_t.shape, lambda i, e: (0, 0), **once),
                  pl.BlockSpec(keys.shape, lambda i, e: (0, 0, 0), **once),
                  pl.BlockSpec((e_tile, d), lambda i, e: (e, 0)),
                  pl.BlockSpec((half, d), lambda i, e: (jnp.minimum(2 * e + 2, 2 * n_steps - 2), 0)),
                  pl.BlockSpec((d, e_tile), lambda i, e: (0, e))],
        out_specs=pl.BlockSpec((tb, d), lambda i, e: (i, 0)),
        out_shape=jax.ShapeDtypeStruct((t, d), F32),
        scratch_shapes=[head_buf(), head_buf(), pltpu.VMEM((nk, tb), F32), head_buf(),
                        pltpu.VMEM((nk, tb), F32),
                        pltpu.VMEM((d, tb), F32),
                        pltpu.VMEM((half, tb), F32), pltpu.VMEM((half, tb), F32),
                        pltpu.VMEM((half, tb), BF16), pltpu.VMEM((half, tb), BF16),
                        pltpu.VMEM((d, half), BF16),
                        pltpu.VMEM((n_ph, 2 * SUBLANES, tb), F32), pltpu.VMEM((n_ph, 2 * SUBLANES, tb), F32),
                        pltpu.VMEM((wq_t.shape[0], tb), BF16)],
        compiler_params=_params(("parallel", "arbitrary")),
        name="peer",
    )(xn2, x2d, wq_t, keys, u_b, u_b, v_t)


def _norm_kernel(x_ref, w_ref, o_ref):
    o_ref[...] = _rms(x_ref[...], w_ref[...])


def _final_norm(x2d, w):
    t, d = x2d.shape
    tm = _pick(t, (1024, 512, 256, 128))
    return pl.pallas_call(
        _norm_kernel,
        grid=(t // tm,),
        in_specs=[pl.BlockSpec((tm, d), lambda i: (i, 0)), pl.BlockSpec((1, d), lambda i: (0, 0))],
        out_specs=pl.BlockSpec((tm, d), lambda i: (i, 0)),
        out_shape=jax.ShapeDtypeStruct((t, d), F32),
        compiler_params=_params(("parallel",)),
        name="final_norm",
    )(x2d, w.reshape(1, d))


def _prep_layer(l, w, dims):
    m_dim, n_mh, s_dim, cd, n_sh, d = dims
    w_in = w["w_in"][l]
    o = [0]
    def take(n):
        o[0] += n
        return w_in[:, o[0] - n:o[0]]
    qkvo, w_ig, w_fg = take(4 * m_dim), take(n_mh), take(n_mh)
    w_z, w_xbc, w_dt, w_g = take(s_dim), take(cd), take(n_sh), take(2 * d)
    n_small = 2 * n_mh + n_sh
    w_small = jnp.concatenate([w_ig, w_fg, w_dt, jnp.zeros((d, SMALL_W - n_small), F32)], axis=1)
    b_small = jnp.concatenate([w["b_igate"][l], w["b_fgate"][l], w["dt_bias"][l],
                               jnp.zeros((SMALL_W - n_small,), F32)]).reshape(1, SMALL_W)
    keys = w["peer_keys"][l]
    return dict(
        norm1=w["norm1_w"][l].reshape(1, d),
        w_big=jnp.concatenate([qkvo, w_xbc, w_z, w_g], axis=1).astype(BF16),
        w_small=w_small.astype(BF16), b_small=b_small,
        w_a=w["w_a"][l].astype(BF16), w_b=w["w_b"][l].astype(BF16), w_out=w["w_out"][l].astype(BF16),
        norm2=w["norm2_w"][l].reshape(1, d),
        wq_t=w["peer_wq"][l].T.astype(BF16),
        keys=keys.reshape((keys.shape[0] * 2,) + keys.shape[2:]).astype(BF16),
        u_b=w["peer_u"][l].astype(BF16), v_t=w["peer_v"][l].T.astype(BF16),
    )


def _trunk(x, c0, n0, m0, ssm0, conv0, w, prepped, dims):
    m_dim, n_mh, s_dim, cd, n_sh, d = dims
    b, s, _ = x.shape
    x2 = x.reshape(b * s, d)
    outs = [[] for _ in range(5)]
    for l, p in enumerate(prepped):
        big2, small2 = _in_proj(x2, p["norm1"], p["w_big"], p["w_small"], p["b_small"])
        big3 = big2.reshape(b, s, -1)
        small3 = small2.reshape(b, s, SMALL_W)
        small_t = small3.transpose(0, 2, 1)
        hm, c, n, m = _mlstm(big3, small_t, c0[l], n0[l], m0[l], w["mlstm_norm_w"][l])
        ys, ssm, conv = _ssd(big3, small3, small_t, conv0[l], ssm0[l], w["conv_w"][l], w["conv_b"][l],
                             w["a_log"][l], w["d_skip"][l], w["ssm_norm_w"][l], m_dim=m_dim, n_mh=n_mh)
        ga_blk = (4 * m_dim + cd + s_dim) // d
        x1, xn2 = _out_proj(x2, hm.reshape(b * s, -1), ys.reshape(b * s, -1), big2,
                            p["w_a"], p["w_b"], p["w_out"], p["norm2"], ga_blk)
        x2 = _peer(xn2, x1, p["wq_t"], p["keys"], p["u_b"], p["v_t"])
        for lst, val in zip(outs, (c, n.reshape(b, n_mh, -1), m.reshape(b, n_mh), ssm, conv)):
            lst.append(val)
    y = _final_norm(x2, w["final_norm_w"]).reshape(b, s, d)
    return (y,) + tuple(jnp.stack(o) for o in outs)


def kernel(x_prompt, x_sample, state_mlstm_C, state_mlstm_n, state_mlstm_m, state_ssm, cache_conv,
           norm1_w, w_in, b_igate, b_fgate, mlstm_norm_w, conv_w, conv_b, dt_bias, a_log, d_skip,
           ssm_norm_w, w_a, w_b, w_out, norm2_w, peer_wq, peer_keys, peer_u, peer_v, final_norm_w):
    w = dict(norm1_w=norm1_w, w_in=w_in, b_igate=b_igate, b_fgate=b_fgate, mlstm_norm_w=mlstm_norm_w,
             conv_w=conv_w, conv_b=conv_b, dt_bias=dt_bias, a_log=a_log, d_skip=d_skip,
             ssm_norm_w=ssm_norm_w, w_a=w_a, w_b=w_b, w_out=w_out, norm2_w=norm2_w, peer_wq=peer_wq,
             peer_keys=peer_keys, peer_u=peer_u, peer_v=peer_v, final_norm_w=final_norm_w)
    depth, _, n_mh, dh, _ = state_mlstm_C.shape
    n_sh, pdim = state_ssm.shape[2], state_ssm.shape[3]
    d = x_prompt.shape[-1]
    dims = (n_mh * dh, n_mh, n_sh * pdim, cache_conv.shape[-1], n_sh, d)
    prepped = [_prep_layer(l, w, dims) for l in range(depth)]
    bp = x_prompt.shape[0]
    zeros = lambda a: jnp.zeros((depth, bp) + a.shape[2:], F32)
    yp = _trunk(x_prompt, zeros(state_mlstm_C), zeros(state_mlstm_n), zeros(state_mlstm_m),
                zeros(state_ssm), zeros(cache_conv), w, prepped, dims)
    ys = _trunk(x_sample, state_mlstm_C, state_mlstm_n, state_mlstm_m, state_ssm, cache_conv,
                w, prepped, dims)
    return (yp[0], ys[0]) + yp[1:] + ys[1:]
```

```python
import functools

import jax
import jax.numpy as jnp
from jax import lax
from jax.experimental import pallas as pl
from jax.experimental.pallas import tpu as pltpu

F32 = jnp.float32
BF16 = jnp.bfloat16
EPS = 1e-6
CONV_W = 4
PEER_TOPK = 16
LANES = 128
SUBLANES = 8
MXU_COLS = 256
SMALL_W = 128
VMEM_LIMIT = 56 * 1024 * 1024
NT_DIMS = (((1,), (1,)), ((), ()))


def _pick(n, prefs):
    for p in prefs:
        if n % p == 0:
            return p
    return n


def _softplus(x):
    return jnp.maximum(x, 0.0) + jnp.log1p(jnp.exp(-jnp.abs(x)))


def _log_sigmoid(x):
    return jnp.minimum(x, 0.0) - jnp.log1p(jnp.exp(-jnp.abs(x)))


def _sigmoid(x):
    return 0.5 * jnp.tanh(0.5 * x) + 0.5


def _silu(x):
    h = 0.5 * x
    return h + h * jnp.tanh(h)


def _split3(x):
    hi = x.astype(BF16)
    r1 = x - hi.astype(F32)
    mid = r1.astype(BF16)
    lo = (r1 - mid.astype(F32)).astype(BF16)
    return hi, mid, lo


def _exact_zero(tile):
    bits = pltpu.bitcast(tile, jnp.uint32)
    sixteen = jnp.uint32(16)
    return pltpu.bitcast(lax.shift_right_logical(lax.shift_right_logical(bits, sixteen), sixteen), F32)


def _rms(x, w):
    return x * lax.rsqrt(jnp.mean(x * x, axis=-1, keepdims=True) + EPS) * w


def _params(sem):
    return pltpu.CompilerParams(dimension_semantics=sem, vmem_limit_bytes=VMEM_LIMIT)


def _in_proj_kernel(x_ref, nw_ref, wbig_ref, wsmall_ref, bsmall_ref, big_ref, small_ref, h_sc):
    @pl.when(pl.program_id(1) == 0)
    def _():
        hb = _rms(x_ref[...], nw_ref[...]).astype(BF16)
        h_sc[...] = hb
        small_ref[...] = jnp.dot(hb, wsmall_ref[...], preferred_element_type=F32) + bsmall_ref[...]

    big_ref[...] = jnp.dot(h_sc[...], wbig_ref[...], preferred_element_type=F32).astype(BF16)


def _in_proj(x2d, nw, w_big, w_small, b_small):
    t, d = x2d.shape
    nbig = w_big.shape[1]
    tm = _pick(t, (1024, 512, 256, 128))
    tn = _pick(nbig, (2048, 1536, 1024, 512, 256, 128))
    return pl.pallas_call(
        _in_proj_kernel,
        grid=(t // tm, nbig // tn),
        in_specs=[
            pl.BlockSpec((tm, d), lambda i, j: (i, 0)),
            pl.BlockSpec((1, d), lambda i, j: (0, 0)),
            pl.BlockSpec((d, tn), lambda i, j: (0, j)),
            pl.BlockSpec((d, SMALL_W), lambda i, j: (0, 0)),
            pl.BlockSpec((1, SMALL_W), lambda i, j: (0, 0)),
        ],
        out_specs=[
            pl.BlockSpec((tm, tn), lambda i, j: (i, j)),
            pl.BlockSpec((tm, SMALL_W), lambda i, j: (i, 0)),
        ],
        out_shape=[jax.ShapeDtypeStruct((t, nbig), BF16), jax.ShapeDtypeStruct((t, SMALL_W), F32)],
        scratch_shapes=[pltpu.VMEM((tm, d), BF16)],
        compiler_params=_params(("parallel", "arbitrary")),
        name="in_proj",
    )(x2d, nw, w_big, w_small, b_small)


def _mlstm_kernel(q_ref, k_ref, v_ref, og_ref, g_ref, c0_ref, n0_ref, m0_ref, nw_ref,
                  h_ref, c_ref, n_ref, m_ref, c_sc, n_sc, m_sc, *, n_heads, scale):
    ci = pl.program_id(1)
    lc = q_ref.shape[1]
    dh = c_sc.shape[2]

    @pl.when(ci == 0)
    def _():
        c_sc[...] = c0_ref[...]
        n_sc[...] = n0_ref[...]
        m_sc[...] = m0_ref[...]

    row = lax.broadcasted_iota(jnp.int32, (lc, lc), 0)
    col = lax.broadcasted_iota(jnp.int32, (lc, lc), 1)
    causal = col <= row
    tri = causal.astype(BF16)
    triu = (row <= col).astype(BF16)
    for bb, hh in [(bb, hh) for bb in range(q_ref.shape[0]) for hh in range(n_heads)]:
        gates = g_ref[bb]
        cols = slice(hh * dh, (hh + 1) * dh)
        ig = gates[hh:hh + 1, :]
        lf = _log_sigmoid(gates[n_heads + hh:n_heads + hh + 1, :])
        lf_parts = _split3(jnp.broadcast_to(lf, (SUBLANES, lc)))
        fcum_col = sum(lax.dot_general(tri, p, NT_DIMS, preferred_element_type=F32)
                       for p in lf_parts)[:, 0:1]
        fcum_row = sum(jnp.dot(p, triu, preferred_element_type=F32) for p in lf_parts)[0:1, :]

        m_prev = m_sc[bb, hh]
        logw = jnp.where(causal, fcum_col - fcum_row + ig, -jnp.inf)
        log_prev = fcum_col + m_prev
        m_t = jnp.maximum(log_prev, jnp.max(logw, axis=1, keepdims=True))
        a_prev = jnp.exp(log_prev - m_t)
        decay = jnp.exp(logw - m_t)

        q = q_ref[bb, :, cols]
        k = k_ref[bb, :, cols]
        v = v_ref[bb, :, cols]
        s = lax.dot_general(q, k, NT_DIMS, preferred_element_type=F32) * scale * decay
        num = (jnp.dot(s.astype(BF16), v, preferred_element_type=F32)
               + a_prev * jnp.dot(q, c_sc[bb, hh].astype(BF16), preferred_element_type=F32))
        qn = jnp.sum(q.astype(F32) * n_sc[bb, hh], axis=1, keepdims=True)
        den = jnp.sum(s, axis=1, keepdims=True) + a_prev * qn
        hout = num / jnp.maximum(jnp.abs(den), jnp.exp(-m_t))
        y = _rms(hout, nw_ref[hh]) * _sigmoid(og_ref[bb, :, cols].astype(F32))
        h_ref[bb, :, cols] = y.astype(BF16)

        f_end = fcum_row[:, lc - 1:lc]
        m_new = m_t[lc - 1:lc, :]
        w_row = jnp.exp(f_end - fcum_row + ig - m_new) * scale
        a_end = jnp.exp(f_end + m_prev - m_new)
        kw = (k.astype(F32).T * w_row).astype(BF16)
        c_sc[bb, hh] = a_end * c_sc[bb, hh] + jnp.dot(kw, v, preferred_element_type=F32)
        w8 = jnp.broadcast_to(w_row, (SUBLANES, lc)).astype(BF16)
        n_sc[bb, hh] = a_end * n_sc[bb, hh] + jnp.dot(w8, k, preferred_element_type=F32)[0:1, :]
        m_sc[bb, hh] = m_new

    @pl.when(ci == pl.num_programs(1) - 1)
    def _():
        c_ref[...] = c_sc[...]
        n_ref[...] = n_sc[...]
        m_ref[...] = m_sc[...]


def _mlstm(big3, small_t, c0, n0, m0, norm_w):
    b, s, _ = big3.shape
    _, nh, dh, _ = c0.shape
    m_dim = nh * dh
    lc = _pick(s, (256, 128, 64, 32, 16))
    bg = _pick(b, (2, 1))
    kern = functools.partial(_mlstm_kernel, n_heads=nh, scale=float(dh) ** -0.5)
    blk = lambda off: pl.BlockSpec((bg, lc, m_dim), lambda bi, ci: (bi, ci, off))
    st4 = lambda r, c: pl.BlockSpec((bg, nh, r, c), lambda bi, ci: (bi, 0, 0, 0))
    return pl.pallas_call(
        kern,
        grid=(b // bg, s // lc),
        in_specs=[blk(0), blk(1), blk(2), blk(3),
                  pl.BlockSpec((bg, SUBLANES, lc), lambda bi, ci: (bi, 0, ci)),
                  st4(dh, dh), st4(1, dh), st4(1, 1),
                  pl.BlockSpec((nh, 1, dh), lambda bi, ci: (0, 0, 0))],
        out_specs=[pl.BlockSpec((bg, lc, m_dim), lambda bi, ci: (bi, ci, 0)),
                   st4(dh, dh), st4(1, dh), st4(1, 1)],
        out_shape=[jax.ShapeDtypeStruct((b, s, m_dim), BF16),
                   jax.ShapeDtypeStruct((b, nh, dh, dh), F32),
                   jax.ShapeDtypeStruct((b, nh, 1, dh), F32),
                   jax.ShapeDtypeStruct((b, nh, 1, 1), F32)],
        scratch_shapes=[pltpu.VMEM((bg, nh, dh, dh), F32), pltpu.VMEM((bg, nh, 1, dh), F32),
                        pltpu.VMEM((bg, nh, 1, 1), F32)],
        compiler_params=_params(("parallel", "arbitrary")),
        name="mlstm",
    )(big3, big3, big3, big3, small_t, c0, n0.reshape(b, nh, 1, dh), m0.reshape(b, nh, 1, 1),
      norm_w.reshape(nh, 1, dh))


def _ssd_kernel(xbc_ref, z_ref, sm_ref, smt_ref, conv0_ref, ssm0_ref, cw_ref, cb_ref,
                alr_ref, alc_ref, dsk_ref, nw_ref, ex_ref,
                ys_ref, ssm_ref, conv_ref, xpad, ht_sc,
                *, dt_off, n_sh, n_groups, n_state, s_dim):
    ci = pl.program_id(1)
    lc = xbc_ref.shape[1]
    gw = s_dim // n_groups
    hpg = n_sh // n_groups
    pdim = gw // hpg
    pad = SUBLANES

    @pl.when(ci == 0)
    def _():
        xpad[pad - (CONV_W - 1):pad, :] = conv0_ref[0]
        for g in range(n_groups):
            ht_sc[g] = ssm0_ref[0, g].T

    xpad[pad:pad + lc, :] = xbc_ref[0].astype(F32)
    xall = xpad[...]
    conv = cb_ref[...] + xall[pad:pad + lc, :] * cw_ref[CONV_W - 1:CONV_W, :]
    for k in range(1, CONV_W):
        conv = conv + pltpu.roll(xall, k, 0)[pad:pad + lc, :] * cw_ref[CONV_W - 1 - k:CONV_W - k, :]
    tail = xpad[pad + lc - (CONV_W - 1):pad + lc, :]
    xpad[pad - (CONV_W - 1):pad, :] = tail
    act = _silu(conv)
    xs = act[:, :s_dim]
    bm = act[:, s_dim:s_dim + n_groups * n_state]
    cm = act[:, s_dim + n_groups * n_state:]

    dt_col = _softplus(sm_ref[0][:, dt_off:dt_off + n_sh])
    dt_row = _softplus(smt_ref[0][dt_off:dt_off + n_sh, :])
    la_col = dt_col * (-jnp.exp(alr_ref[...]))
    la_row = dt_row * (-jnp.exp(alc_ref[...]))
    row = lax.broadcasted_iota(jnp.int32, (lc, lc), 0)
    col = lax.broadcasted_iota(jnp.int32, (lc, lc), 1)
    causal = col <= row
    tri = causal.astype(BF16)
    triu = (row <= col).astype(BF16)
    cum_col = sum(jnp.dot(tri, p, preferred_element_type=F32) for p in _split3(la_col))
    cum_row = sum(jnp.dot(p, triu, preferred_element_type=F32) for p in _split3(la_row))
    w_end = jnp.exp(cum_col[lc - 1:lc, :] - cum_col) * dt_col
    ex3 = ex_ref[...]
    ecum_x = jnp.dot(jnp.concatenate(_split3(jnp.exp(cum_col)), axis=1), ex3, preferred_element_type=F32)
    wend_x = jnp.dot(jnp.concatenate(_split3(w_end), axis=1), ex3, preferred_element_type=F32)
    xw = (xs * wend_x).astype(BF16)
    xs_b = xs.astype(BF16)
    head_of_lane = lax.div(lax.broadcasted_iota(jnp.int32, (1, gw), 1), pdim)

    parts = []
    for g in range(n_groups):
        bm_g = bm[:, g * n_state:(g + 1) * n_state]
        cm_b = cm[:, g * n_state:(g + 1) * n_state].astype(BF16)
        cb = lax.dot_general(cm_b, bm_g.astype(BF16), NT_DIMS, preferred_element_type=F32)
        xg = xs_b[:, g * gw:(g + 1) * gw]
        yg = (jnp.dot(cm_b, ht_sc[g].astype(BF16), preferred_element_type=F32)
              * ecum_x[:, g * gw:(g + 1) * gw])
        for kk in range(hpg):
            hd = g * hpg + kk
            dec = jnp.exp(jnp.where(causal, cum_col[:, hd:hd + 1] - cum_row[hd:hd + 1, :], -jnp.inf))
            wm = (cb * dec * dt_row[hd:hd + 1, :]).astype(BF16)
            xm = jnp.where(head_of_lane == kk, xg, jnp.zeros_like(xg))
            yg = yg + jnp.dot(wm, xm, preferred_element_type=F32)
        ht_sc[g] = (ecum_x[lc - 1:lc, g * gw:(g + 1) * gw] * ht_sc[g]
                    + jnp.dot(bm_g.T.astype(BF16), xw[:, g * gw:(g + 1) * gw], preferred_element_type=F32))
        parts.append(yg)
    y = jnp.concatenate(parts, axis=1) + dsk_ref[...] * xs
    zf = z_ref[0].astype(F32)
    y = y * _silu(zf)
    ys_ref[0] = _rms(y, nw_ref[...]).astype(BF16)

    @pl.when(ci == pl.num_programs(1) - 1)
    def _():
        conv_ref[0] = tail
        for g in range(n_groups):
            ssm_ref[0, g] = ht_sc[g].T


def _ssd(big3, small3, small_t, conv0, ssm0, conv_w, conv_b, a_log, d_skip, norm_w, *, m_dim, n_mh):
    b, s, _ = big3.shape
    _, n_sh, pdim, n_state = ssm0.shape
    cd = conv0.shape[2]
    s_dim = n_sh * pdim
    n_groups = (cd - s_dim) // (2 * n_state)
    gw = s_dim // n_groups
    lc = _pick(s, (256, 128, 64, 32, 16))
    xbc_blk = (4 * m_dim) // cd
    z_blk = (4 * m_dim + cd) // s_dim
    assert xbc_blk * cd == 4 * m_dim and z_blk * s_dim == 4 * m_dim + cd
    kern = functools.partial(_ssd_kernel, dt_off=2 * n_mh, n_sh=n_sh, n_groups=n_groups,
                             n_state=n_state, s_dim=s_dim)
    full2 = lambda r, c: pl.BlockSpec((r, c), lambda bi, ci: (0, 0))
    expander = jnp.tile(jnp.repeat(jnp.eye(n_sh, dtype=BF16), pdim, axis=1), (3, 1))
    ys, ssm, conv = pl.pallas_call(
        kern,
        grid=(b, s // lc),
        in_specs=[pl.BlockSpec((1, lc, cd), lambda bi, ci: (bi, ci, xbc_blk)),
                  pl.BlockSpec((1, lc, s_dim), lambda bi, ci: (bi, ci, z_blk)),
                  pl.BlockSpec((1, lc, SMALL_W), lambda bi, ci: (bi, ci, 0)),
                  pl.BlockSpec((1, SMALL_W, lc), lambda bi, ci: (bi, 0, ci)),
                  pl.BlockSpec((1, CONV_W - 1, cd), lambda bi, ci: (bi, 0, 0)),
                  pl.BlockSpec((1, n_groups, gw, n_state), lambda bi, ci: (bi, 0, 0, 0)),
                  full2(CONV_W, cd), full2(1, cd), full2(1, n_sh), full2(n_sh, 1),
                  full2(1, s_dim), full2(1, s_dim), full2(3 * n_sh, s_dim)],
        out_specs=[pl.BlockSpec((1, lc, s_dim), lambda bi, ci: (bi, ci, 0)),
                   pl.BlockSpec((1, n_groups, gw, n_state), lambda bi, ci: (bi, 0, 0, 0)),
                   pl.BlockSpec((1, CONV_W - 1, cd), lambda bi, ci: (bi, 0, 0))],
        out_shape=[jax.ShapeDtypeStruct((b, s, s_dim), BF16),
                   jax.ShapeDtypeStruct((b, n_groups, gw, n_state), F32),
                   jax.ShapeDtypeStruct((b, CONV_W - 1, cd), F32)],
        scratch_shapes=[pltpu.VMEM((lc + SUBLANES, cd), F32), pltpu.VMEM((n_groups, n_state, gw), F32)],
        compiler_params=_params(("parallel", "arbitrary")),
        name="ssd",
    )(big3, big3, small3, small_t, conv0, ssm0.reshape(b, n_groups, gw, n_state), conv_w,
      conv_b.reshape(1, cd), a_log.reshape(1, n_sh), a_log.reshape(n_sh, 1),
      jnp.repeat(d_skip, pdim).reshape(1, s_dim), norm_w.reshape(1, s_dim), expander)
    return ys, ssm.reshape(b, n_sh, pdim, n_state), conv


def _out_kernel(x_ref, hm_ref, ys_ref, ga_ref, gb_ref, wa_ref, wb_ref, wo_ref, n2_ref, xo_ref, xn_ref):
    a = jnp.dot(hm_ref[...], wa_ref[...], preferred_element_type=F32)
    bb = jnp.dot(ys_ref[...], wb_ref[...], preferred_element_type=F32)
    mix = _sigmoid(ga_ref[...].astype(F32)) * a + _sigmoid(gb_ref[...].astype(F32)) * bb
    xo = x_ref[...] + jnp.dot(mix.astype(BF16), wo_ref[...], preferred_element_type=F32)
    xo_ref[...] = xo
    xn_ref[...] = _rms(xo, n2_ref[...]).astype(BF16)


def _out_proj(x2d, hm2, ys2, big2, w_a, w_b, w_out, norm2_w, ga_blk):
    t, d = x2d.shape
    tm = _pick(t, (512, 256, 128))
    full = lambda a: pl.BlockSpec(a.shape, lambda i: (0, 0))
    rows = lambda w: pl.BlockSpec((tm, w), lambda i: (i, 0))
    return pl.pallas_call(
        _out_kernel,
        grid=(t // tm,),
        in_specs=[rows(d), rows(hm2.shape[1]), rows(ys2.shape[1]),
                  pl.BlockSpec((tm, d), lambda i: (i, ga_blk)),
                  pl.BlockSpec((tm, d), lambda i: (i, ga_blk + 1)),
                  full(w_a), full(w_b), full(w_out), full(norm2_w)],
        out_specs=[rows(d), rows(d)],
        out_shape=[jax.ShapeDtypeStruct((t, d), F32), jax.ShapeDtypeStruct((t, d), BF16)],
        compiler_params=_params(("parallel",)),
        name="out_proj",
    )(x2d, hm2, ys2, big2, big2, w_a, w_b, w_out, norm2_w)


N_TOP = PEER_TOPK + 1
GELU_C1 = (2.0 / 3.141592653589793) ** 0.5
GELU_C2 = GELU_C1 * 0.044715


def _sort_network(n):
    def merge(lo, hi, r):
        step = 2 * r
        if step < hi - lo:
            yield from merge(lo, hi, step)
            yield from merge(lo + r, hi, step)
            yield from ((i, i + r) for i in range(lo + r, hi - r, step))
        else:
            yield (lo, lo + r)

    def sort(lo, hi):
        if hi > lo:
            mid = lo + (hi - lo) // 2
            yield from sort(lo, mid)
            yield from sort(mid + 1, hi)
            yield from merge(lo, hi, 1)

    return tuple(sort(0, n - 1))


def _all_sublanes(op, x):
    shift = SUBLANES // 2
    while shift:
        x = op(x, pltpu.roll(x, shift, 0))
        shift //= 2
    return x


def _merge_columns(v):
    n = len(v)
    dropped = None
    shift = SUBLANES // 2
    while shift:
        partner = [pltpu.roll(v[n - 1 - k], shift, 0) for k in range(n)]
        lo = functools.reduce(jnp.maximum, [jnp.minimum(v[k], partner[k]) for k in range(n)])
        dropped = lo if dropped is None else jnp.maximum(dropped, lo)
        v = [jnp.maximum(v[k], partner[k]) for k in range(n)]
        stride = n // 2
        while stride:
            for i in range(n):
                if not i & stride:
                    v[i], v[i + stride] = jnp.maximum(v[i], v[i + stride]), jnp.minimum(v[i], v[i + stride])
            stride //= 2
        shift //= 2
    return v, dropped


def _top_values(tiles):
    v = list(tiles)
    assert len(v) == PEER_TOPK
    for i, j in _sort_network(len(v)):
        v[i], v[j] = jnp.maximum(v[i], v[j]), jnp.minimum(v[i], v[j])
    top, dropped = _merge_columns(v)
    return top + [_all_sublanes(jnp.maximum, dropped)]


def _pair_threshold(a, b):
    sub = lax.broadcasted_iota(jnp.int32, (SUBLANES, LANES), 0)
    b0 = b[SUBLANES - 1]
    for s in reversed(range(SUBLANES - 1)):
        b0 = jnp.where(sub == s, b[s], b0)
    top, dropped = _merge_columns([b0 + a[k] for k in range(PEER_TOPK)])
    t17 = _all_sublanes(jnp.maximum, dropped)
    x = a[PEER_TOPK] + b[0]
    extra = []
    for l in range(SUBLANES, N_TOP):
        y = a[0] + b[l]
        extra.append(jnp.maximum(y, x))
        x = jnp.minimum(y, x)
    extra.append(x)
    sel = list(top)
    for j, e in enumerate(extra):
        k = PEER_TOPK - 1 - j
        t17 = jnp.maximum(t17, jnp.minimum(top[k], e))
        sel[k] = jnp.maximum(top[k], e)
    v16 = functools.reduce(jnp.minimum, sel[PEER_TOPK - 1 - len(extra):])
    return sel, v16, t17


def _peer_kernel(xn_ref, x_ref, wq_ref, keys_ref, u_ref, unext_ref, vt_ref, out_ref,
                 tau_sc, e0_sc, s1_sc, e1_sc, s0_sc, acc_sc,
                 sa_sc, sb_sc, wa_sc, wb_sc, vprev_sc, trow_sc, erow_sc, qt_sc, *, n_ph):
    et = pl.program_id(1)
    tb = xn_ref.shape[0]
    nk = keys_ref.shape[1]
    phalf = keys_ref.shape[2]
    e_tile = u_ref.shape[0]

    @pl.when(et == 0)
    def _():
        acc_sc[...] = jnp.zeros_like(acc_sc)
        wb_sc[...] = jnp.zeros_like(wb_sc)
        vprev_sc[...] = jnp.zeros_like(vprev_sc)
        xn = xn_ref[...]
        q_rows = wq_ref.shape[0] // 2
        for c in range(2):
            rows = slice(c * q_rows, (c + 1) * q_rows)
            qt_sc[rows, :] = lax.dot_general(wq_ref[rows, :], xn, NT_DIMS,
                                             preferred_element_type=F32).astype(BF16)

        def head(h, carry):
            for c, dst in ((0, s0_sc), (1, s1_sc)):
                hc = 2 * h + c
                q_t = qt_sc[pl.ds(pl.multiple_of(hc * phalf, phalf), phalf), :]
                dst[...] = jnp.dot(keys_ref[hc], q_t, preferred_element_type=F32)
            for lc in range(tb // LANES):
                lanes = slice(lc * LANES, (lc + 1) * LANES)
                rows = [slice(k * SUBLANES, (k + 1) * SUBLANES) for k in range(nk // SUBLANES)]
                s0 = [s0_sc[r, lanes] for r in rows]
                s1 = [s1_sc[r, lanes] for r in rows]
                a = _top_values(s0)
                b = _top_values(s1)
                sel, v16, v17 = _pair_threshold(a, b)
                tau = 0.5 * (v16 + v17)
                z = functools.reduce(jnp.add, [jnp.exp(t - sel[0]) for t in sel])
                rz = 0.5 / z
                tb1 = tau - b[0]
                for r, t0, t1 in zip(rows, s0, s1):
                    tau_sc[h, r, lanes] = jnp.exp(tb1 - t0)
                    e0_sc[h, r, lanes] = jnp.exp(t0 - a[0]) * rz
                    e1_sc[h, r, lanes] = jnp.exp(t1 - b[0])
            return carry

        lax.fori_loop(0, n_ph, head, 0)
        sa_sc[...] = lax.dot_general(u_ref[0:e_tile // 2, :], xn, NT_DIMS, preferred_element_type=F32)

    half = e_tile // 2

    base = pl.multiple_of(et * 2 * SUBLANES, 2 * SUBLANES)
    for h in range(n_ph):
        trow_sc[h] = tau_sc[h, pl.ds(base, 2 * SUBLANES), :]
        erow_sc[h] = e0_sc[h, pl.ds(base, 2 * SUBLANES), :]

    def gates(tile, s_in, w_out, lc, anchor=None):
        rows = slice(tile * SUBLANES, (tile + 1) * SUBLANES)
        lanes = slice(lc * LANES, (lc + 1) * LANES)
        n_jt = nk // SUBLANES
        last = None
        for r in range(SUBLANES):
            g = [None] * n_jt
            for h in range(n_ph):
                t0 = jnp.broadcast_to(trow_sc[h, rows, lanes][r:r + 1], (SUBLANES, LANES))
                ee = jnp.broadcast_to(erow_sc[h, rows, lanes][r:r + 1], (SUBLANES, LANES))
                for jt in range(n_jt):
                    x1 = e1_sc[h, jt * SUBLANES:(jt + 1) * SUBLANES, lanes]
                    term = jnp.where(x1 >= t0, x1, 0.0) * ee
                    g[jt] = term if g[jt] is None else g[jt] + term
            for jp in range(n_jt // 2):
                r0 = r * nk + jp * 2 * SUBLANES
                w_pair = []
                for jj in range(2):
                    sv = s_in[r0 + jj * SUBLANES:r0 + (jj + 1) * SUBLANES, lanes]
                    if anchor is not None:
                        sv, anchor = sv + anchor, None
                    inner = sv * (GELU_C1 + GELU_C2 * (sv * sv))
                    w_tile = g[2 * jp + jj] * (sv + sv * jnp.tanh(inner))
                    w_pair.append(w_tile)
                    last = w_tile if last is None else jnp.maximum(last, w_tile)
                w_out[r0:r0 + 2 * SUBLANES, lanes] = jnp.concatenate(w_pair, axis=0).astype(BF16)
        return last

    def held(operand, after):
        if after is None:
            return operand
        return operand + _exact_zero(after)[0:1, 0:1].astype(operand.dtype)

    def acc_piece(v_ref, w_ref, after):
        res = acc_sc[...] + jnp.dot(v_ref[...], held(w_ref[...], after), preferred_element_type=F32)
        acc_sc[...] = res
        return res[-SUBLANES:, -LANES:]

    def score_piece(s_ref, rows_ref, after):
        res = lax.dot_general(rows_ref[...], held(xn_ref[...], after), NT_DIMS,
                              preferred_element_type=F32)
        s_ref[...] = res
        return res[-SUBLANES:, -LANES:]

    ub_ref = u_ref.at[half:, :]
    va_ref = vt_ref.at[:, 0:half]
    n_lc = tb // LANES
    n_sec = 2 * n_lc
    mxu_work = [(lambda a: acc_piece(vprev_sc, wb_sc, a), 0, n_lc // 2),
                (lambda a: score_piece(sb_sc, ub_ref, a), min(1, n_lc - 1), n_lc),
                (lambda a: acc_piece(va_ref, wa_sc, a), n_lc, n_lc + n_lc // 2),
                (lambda a: score_piece(sa_sc, unext_ref, a), n_lc + 1, n_sec - 1)]
    anchored = n_lc >= 4
    sec_out, due = [], {}
    for sec in range(n_sec):
        tile, lc = divmod(sec, n_lc)
        for fn, at, by in mxu_work:
            if at == sec:
                after = sec_out[sec - 2] if anchored and sec >= 2 else None
                due[by] = fn(after)
        anchor = _exact_zero(due[sec]) if anchored and sec in due and sec > 0 else None
        sec_out.append(gates(tile, (sa_sc, sb_sc)[tile], (wa_sc, wb_sc)[tile], lc, anchor))
    vprev_sc[...] = vt_ref[:, half:]

    @pl.when(et == pl.num_programs(1) - 1)
    def _():
        tail = jnp.dot(vprev_sc[...], wb_sc[...], preferred_element_type=F32)
        out_ref[...] = x_ref[...] + (acc_sc[...] + tail).T


def _peer(xn2, x2d, wq_t, keys, u_b, v_t):
    t, d = x2d.shape
    n_hc, nk, phalf = keys.shape
    n_ph = n_hc // 2
    n_exp = u_b.shape[0]
    tb = _pick(t, (512, 256))
    half = SUBLANES * nk
    e_tile = 2 * half
    assert n_exp % e_tile == 0 and tb % MXU_COLS == 0
    n_steps = n_exp // e_tile
    kern = functools.partial(_peer_kernel, n_ph=n_ph)
    head_buf = lambda: pltpu.VMEM((n_ph, nk, tb), F32)
    once = dict(pipeline_mode=pl.Buffered(1))
    return pl.pallas_call(
        kern,
        grid=(t // tb, n_exp // e_tile),
        in_specs=[pl.BlockSpec((tb, d), lambda i, e: (i, 0), **once),
                  pl.BlockSpec((tb, d), lambda i, e: (i, 0), **once),
                  pl.BlockSpec(wq_t.shape, lambda i, e: (0, 0), **once),
                  pl.BlockSpec(keys.shape, lambda i, e: (0, 0, 0), **once),
                  pl.BlockSpec((e_tile, d), lambda i, e: (e, 0)),
                  pl.BlockSpec((half, d), lambda i, e: (jnp.minimum(2 * e + 2, 2 * n_steps - 2), 0)),
                  pl.BlockSpec((d, e_tile), lambda i, e: (0, e))],
        out_specs=pl.BlockSpec((tb, d), lambda i, e: (i, 0)),
        out_shape=jax.ShapeDtypeStruct((t, d), F32),
        scratch_shapes=[head_buf(), head_buf(), pltpu.VMEM((nk, tb), F32), head_buf(),
                        pltpu.VMEM((nk, tb), F32),
                        pltpu.VMEM((d, tb), F32),
                        pltpu.VMEM((half, tb), F32), pltpu.VMEM((half, tb), F32),
                        pltpu.VMEM((half, tb), BF16), pltpu.VMEM((half, tb), BF16),
                        pltpu.VMEM((d, half), BF16),
                        pltpu.VMEM((n_ph, 2 * SUBLANES, tb), F32), pltpu.VMEM((n_ph, 2 * SUBLANES, tb), F32),
                        pltpu.VMEM((wq_t.shape[0], tb), BF16)],
        compiler_params=_params(("parallel", "arbitrary")),
        name="peer",
    )(xn2, x2d, wq_t, keys, u_b, u_b, v_t)


def _norm_kernel(x_ref, w_ref, o_ref):
    o_ref[...] = _rms(x_ref[...], w_ref[...])


def _final_norm(x2d, w):
    t, d = x2d.shape
    tm = _pick(t, (1024, 512, 256, 128))
    return pl.pallas_call(
        _norm_kernel,
        grid=(t // tm,),
        in_specs=[pl.BlockSpec((tm, d), lambda i: (i, 0)), pl.BlockSpec((1, d), lambda i: (0, 0))],
        out_specs=pl.BlockSpec((tm, d), lambda i: (i, 0)),
        out_shape=jax.ShapeDtypeStruct((t, d), F32),
        compiler_params=_params(("parallel",)),
        name="final_norm",
    )(x2d, w.reshape(1, d))


def _prep_layer(l, w, dims):
    m_dim, n_mh, s_dim, cd, n_sh, d = dims
    w_in = w["w_in"][l]
    o = [0]
    def take(n):
        o[0] += n
        return w_in[:, o[0] - n:o[0]]
    qkvo, w_ig, w_fg = take(4 * m_dim), take(n_mh), take(n_mh)
    w_z, w_xbc, w_dt, w_g = take(s_dim), take(cd), take(n_sh), take(2 * d)
    n_small = 2 * n_mh + n_sh
    w_small = jnp.concatenate([w_ig, w_fg, w_dt, jnp.zeros((d, SMALL_W - n_small), F32)], axis=1)
    b_small = jnp.concatenate([w["b_igate"][l], w["b_fgate"][l], w["dt_bias"][l],
                               jnp.zeros((SMALL_W - n_small,), F32)]).reshape(1, SMALL_W)
    keys = w["peer_keys"][l]
    return dict(
        norm1=w["norm1_w"][l].reshape(1, d),
        w_big=jnp.concatenate([qkvo, w_xbc, w_z, w_g], axis=1).astype(BF16),
        w_small=w_small.astype(BF16), b_small=b_small,
        w_a=w["w_a"][l].astype(BF16), w_b=w["w_b"][l].astype(BF16), w_out=w["w_out"][l].astype(BF16),
        norm2=w["norm2_w"][l].reshape(1, d),
        wq_t=w["peer_wq"][l].T.astype(BF16),
        keys=keys.reshape((keys.shape[0] * 2,) + keys.shape[2:]).astype(BF16),
        u_b=w["peer_u"][l].astype(BF16), v_t=w["peer_v"][l].T.astype(BF16),
    )


def _trunk(x, c0, n0, m0, ssm0, conv0, w, prepped, dims):
    m_dim, n_mh, s_dim, cd, n_sh, d = dims
    b, s, _ = x.shape
    x2 = x.reshape(b * s, d)
    outs = [[] for _ in range(5)]
    for l, p in enumerate(prepped):
        big2, small2 = _in_proj(x2, p["norm1"], p["w_big"], p["w_small"], p["b_small"])
        big3 = big2.reshape(b, s, -1)
        small3 = small2.reshape(b, s, SMALL_W)
        small_t = small3.transpose(0, 2, 1)
        hm, c, n, m = _mlstm(big3, small_t, c0[l], n0[l], m0[l], w["mlstm_norm_w"][l])
        ys, ssm, conv = _ssd(big3, small3, small_t, conv0[l], ssm0[l], w["conv_w"][l], w["conv_b"][l],
                             w["a_log"][l], w["d_skip"][l], w["ssm_norm_w"][l], m_dim=m_dim, n_mh=n_mh)
        ga_blk = (4 * m_dim + cd + s_dim) // d
        x1, xn2 = _out_proj(x2, hm.reshape(b * s, -1), ys.reshape(b * s, -1), big2,
                            p["w_a"], p["w_b"], p["w_out"], p["norm2"], ga_blk)
        x2 = _peer(xn2, x1, p["wq_t"], p["keys"], p["u_b"], p["v_t"])
        for lst, val in zip(outs, (c, n.reshape(b, n_mh, -1), m.reshape(b, n_mh), ssm, conv)):
            lst.append(val)
    y = _final_norm(x2, w["final_norm_w"]).reshape(b, s, d)
    return (y,) + tuple(jnp.stack(o) for o in outs)


def kernel(x_prompt, x_sample, state_mlstm_C, state_mlstm_n, state_mlstm_m, state_ssm, cache_conv,
           norm1_w, w_in, b_igate, b_fgate, mlstm_norm_w, conv_w, conv_b, dt_bias, a_log, d_skip,
           ssm_norm_w, w_a, w_b, w_out, norm2_w, peer_wq, peer_keys, peer_u, peer_v, final_norm_w):
    w = dict(norm1_w=norm1_w, w_in=w_in, b_igate=b_igate, b_fgate=b_fgate, mlstm_norm_w=mlstm_norm_w,
             conv_w=conv_w, conv_b=conv_b, dt_bias=dt_bias, a_log=a_log, d_skip=d_skip,
             ssm_norm_w=ssm_norm_w, w_a=w_a, w_b=w_b, w_out=w_out, norm2_w=norm2_w, peer_wq=peer_wq,
             peer_keys=peer_keys, peer_u=peer_u, peer_v=peer_v, final_norm_w=final_norm_w)
    depth, _, n_mh, dh, _ = state_mlstm_C.shape
    n_sh, pdim = state_ssm.shape[2], state_ssm.shape[3]
    d = x_prompt.shape[-1]
    dims = (n_mh * dh, n_mh, n_sh * pdim, cache_conv.shape[-1], n_sh, d)
    prepped = [_prep_layer(l, w, dims) for l in range(depth)]
    bp = x_prompt.shape[0]
    zeros = lambda a: jnp.zeros((depth, bp) + a.shape[2:], F32)
    yp = _trunk(x_prompt, zeros(state_mlstm_C), zeros(state_mlstm_n), zeros(state_mlstm_m),
                zeros(state_ssm), zeros(cache_conv), w, prepped, dims)
    ys = _trunk(x_sample, state_mlstm_C, state_mlstm_n, state_mlstm_m, state_ssm, cache_conv,
                w, prepped, dims)
    return (yp[0], ys[0]) + yp[1:] + ys[1:]
```

```python
import functools

import jax
import jax.numpy as jnp
from jax import lax
from jax.experimental import pallas as pl
from jax.experimental.pallas import tpu as pltpu

F32 = jnp.float32
BF16 = jnp.bfloat16
EPS = 1e-6
CONV_W = 4
PEER_TOPK = 16
LANES = 128
SUBLANES = 8
MXU_COLS = 256
SMALL_W = 128
VMEM_LIMIT = 56 * 1024 * 1024
NT_DIMS = (((1,), (1,)), ((), ()))


def _pick(n, prefs):
    for p in prefs:
        if n % p == 0:
            return p
    return n


def _softplus(x):
    return jnp.maximum(x, 0.0) + jnp.log1p(jnp.exp(-jnp.abs(x)))


def _log_sigmoid(x):
    return jnp.minimum(x, 0.0) - jnp.log1p(jnp.exp(-jnp.abs(x)))


def _sigmoid(x):
    return 0.5 * jnp.tanh(0.5 * x) + 0.5


def _silu(x):
    h = 0.5 * x
    return h + h * jnp.tanh(h)


def _split3(x):
    hi = x.astype(BF16)
    r1 = x - hi.astype(F32)
    mid = r1.astype(BF16)
    lo = (r1 - mid.astype(F32)).astype(BF16)
    return hi, mid, lo


def _exact_zero(tile):
    bits = pltpu.bitcast(tile, jnp.uint32)
    sixteen = jnp.uint32(16)
    return pltpu.bitcast(lax.shift_right_logical(lax.shift_right_logical(bits, sixteen), sixteen), F32)


def _rms(x, w):
    return x * lax.rsqrt(jnp.mean(x * x, axis=-1, keepdims=True) + EPS) * w


def _params(sem):
    return pltpu.CompilerParams(dimension_semantics=sem, vmem_limit_bytes=VMEM_LIMIT)


def _in_proj_kernel(x_ref, nw_ref, wbig_ref, wsmall_ref, bsmall_ref, big_ref, small_ref, h_sc):
    @pl.when(pl.program_id(1) == 0)
    def _():
        hb = _rms(x_ref[...], nw_ref[...]).astype(BF16)
        h_sc[...] = hb
        small_ref[...] = jnp.dot(hb, wsmall_ref[...], preferred_element_type=F32) + bsmall_ref[...]

    big_ref[...] = jnp.dot(h_sc[...], wbig_ref[...], preferred_element_type=F32).astype(BF16)


def _in_proj(x2d, nw, w_big, w_small, b_small):
    t, d = x2d.shape
    nbig = w_big.shape[1]
    tm = _pick(t, (1024, 512, 256, 128))
    tn = _pick(nbig, (2048, 1536, 1024, 512, 256, 128))
    return pl.pallas_call(
        _in_proj_kernel,
        grid=(t // tm, nbig // tn),
        in_specs=[
            pl.BlockSpec((tm, d), lambda i, j: (i, 0)),
            pl.BlockSpec((1, d), lambda i, j: (0, 0)),
            pl.BlockSpec((d, tn), lambda i, j: (0, j)),
            pl.BlockSpec((d, SMALL_W), lambda i, j: (0, 0)),
            pl.BlockSpec((1, SMALL_W), lambda i, j: (0, 0)),
        ],
        out_specs=[
            pl.BlockSpec((tm, tn), lambda i, j: (i, j)),
            pl.BlockSpec((tm, SMALL_W), lambda i, j: (i, 0)),
        ],
        out_shape=[jax.ShapeDtypeStruct((t, nbig), BF16), jax.ShapeDtypeStruct((t, SMALL_W), F32)],
        scratch_shapes=[pltpu.VMEM((tm, d), BF16)],
        compiler_params=_params(("parallel", "arbitrary")),
        name="in_proj",
    )(x2d, nw, w_big, w_small, b_small)


def _mlstm_kernel(q_ref, k_ref, v_ref, og_ref, g_ref, c0_ref, n0_ref, m0_ref, nw_ref,
                  h_ref, c_ref, n_ref, m_ref, c_sc, n_sc, m_sc, *, n_heads, scale):
    ci = pl.program_id(1)
    lc = q_ref.shape[1]
    dh = c_sc.shape[2]

    @pl.when(ci == 0)
    def _():
        c_sc[...] = c0_ref[...]
        n_sc[...] = n0_ref[...]
        m_sc[...] = m0_ref[...]

    row = lax.broadcasted_iota(jnp.int32, (lc, lc), 0)
    col = lax.broadcasted_iota(jnp.int32, (lc, lc), 1)
    causal = col <= row
    tri = causal.astype(BF16)
    triu = (row <= col).astype(BF16)
    for bb, hh in [(bb, hh) for bb in range(q_ref.shape[0]) for hh in range(n_heads)]:
        gates = g_ref[bb]
        cols = slice(hh * dh, (hh + 1) * dh)
        ig = gates[hh:hh + 1, :]
        lf = _log_sigmoid(gates[n_heads + hh:n_heads + hh + 1, :])
        lf_parts = _split3(jnp.broadcast_to(lf, (SUBLANES, lc)))
        fcum_col = sum(lax.dot_general(tri, p, NT_DIMS, preferred_element_type=F32)
                       for p in lf_parts)[:, 0:1]
        fcum_row = sum(jnp.dot(p, triu, preferred_element_type=F32) for p in lf_parts)[0:1, :]

        m_prev = m_sc[bb, hh]
        logw = jnp.where(causal, fcum_col - fcum_row + ig, -jnp.inf)
        log_prev = fcum_col + m_prev
        m_t = jnp.maximum(log_prev, jnp.max(logw, axis=1, keepdims=True))
        a_prev = jnp.exp(log_prev - m_t)
        decay = jnp.exp(logw - m_t)

        q = q_ref[bb, :, cols]
        k = k_ref[bb, :, cols]
        v = v_ref[bb, :, cols]
        s = lax.dot_general(q, k, NT_DIMS, preferred_element_type=F32) * scale * decay
        num = (jnp.dot(s.astype(BF16), v, preferred_element_type=F32)
               + a_prev * jnp.dot(q, c_sc[bb, hh].astype(BF16), preferred_element_type=F32))
        qn = jnp.sum(q.astype(F32) * n_sc[bb, hh], axis=1, keepdims=True)
        den = jnp.sum(s, axis=1, keepdims=True) + a_prev * qn
        hout = num / jnp.maximum(jnp.abs(den), jnp.exp(-m_t))
        y = _rms(hout, nw_ref[hh]) * _sigmoid(og_ref[bb, :, cols].astype(F32))
        h_ref[bb, :, cols] = y.astype(BF16)

        f_end = fcum_row[:, lc - 1:lc]
        m_new = m_t[lc - 1:lc, :]
        w_row = jnp.exp(f_end - fcum_row + ig - m_new) * scale
        a_end = jnp.exp(f_end + m_prev - m_new)
        kw = (k.astype(F32).T * w_row).astype(BF16)
        c_sc[bb, hh] = a_end * c_sc[bb, hh] + jnp.dot(kw, v, preferred_element_type=F32)
        w8 = jnp.broadcast_to(w_row, (SUBLANES, lc)).astype(BF16)
        n_sc[bb, hh] = a_end * n_sc[bb, hh] + jnp.dot(w8, k, preferred_element_type=F32)[0:1, :]
        m_sc[bb, hh] = m_new

    @pl.when(ci == pl.num_programs(1) - 1)
    def _():
        c_ref[...] = c_sc[...]
        n_ref[...] = n_sc[...]
        m_ref[...] = m_sc[...]


def _mlstm(big3, small_t, c0, n0, m0, norm_w):
    b, s, _ = big3.shape
    _, nh, dh, _ = c0.shape
    m_dim = nh * dh
    lc = _pick(s, (256, 128, 64, 32, 16))
    bg = _pick(b, (2, 1))
    kern = functools.partial(_mlstm_kernel, n_heads=nh, scale=float(dh) ** -0.5)
    blk = lambda off: pl.BlockSpec((bg, lc, m_dim), lambda bi, ci: (bi, ci, off))
    st4 = lambda r, c: pl.BlockSpec((bg, nh, r, c), lambda bi, ci: (bi, 0, 0, 0))
    return pl.pallas_call(
        kern,
        grid=(b // bg, s // lc),
        in_specs=[blk(0), blk(1), blk(2), blk(3),
                  pl.BlockSpec((bg, SUBLANES, lc), lambda bi, ci: (bi, 0, ci)),
                  st4(dh, dh), st4(1, dh), st4(1, 1),
                  pl.BlockSpec((nh, 1, dh), lambda bi, ci: (0, 0, 0))],
        out_specs=[pl.BlockSpec((bg, lc, m_dim), lambda bi, ci: (bi, ci, 0)),
                   st4(dh, dh), st4(1, dh), st4(1, 1)],
        out_shape=[jax.ShapeDtypeStruct((b, s, m_dim), BF16),
                   jax.ShapeDtypeStruct((b, nh, dh, dh), F32),
                   jax.ShapeDtypeStruct((b, nh, 1, dh), F32),
                   jax.ShapeDtypeStruct((b, nh, 1, 1), F32)],
        scratch_shapes=[pltpu.VMEM((bg, nh, dh, dh), F32), pltpu.VMEM((bg, nh, 1, dh), F32),
                        pltpu.VMEM((bg, nh, 1, 1), F32)],
        compiler_params=_params(("parallel", "arbitrary")),
        name="mlstm",
    )(big3, big3, big3, big3, small_t, c0, n0.reshape(b, nh, 1, dh), m0.reshape(b, nh, 1, 1),
      norm_w.reshape(nh, 1, dh))


def _ssd_kernel(xbc_ref, z_ref, sm_ref, smt_ref, conv0_ref, ssm0_ref, cw_ref, cb_ref,
                alr_ref, alc_ref, dsk_ref, nw_ref, ex_ref,
                ys_ref, ssm_ref, conv_ref, xpad, ht_sc,
                *, dt_off, n_sh, n_groups, n_state, s_dim):
    ci = pl.program_id(1)
    lc = xbc_ref.shape[1]
    gw = s_dim // n_groups
    hpg = n_sh // n_groups
    pdim = gw // hpg
    pad = SUBLANES

    @pl.when(ci == 0)
    def _():
        xpad[0:pad, :] = jnp.zeros((pad, xpad.shape[1]), F32)
        xpad[pad - (CONV_W - 1):pad, :] = conv0_ref[0]
        for g in range(n_groups):
            ht_sc[g] = ssm0_ref[0, g].T

    xpad[pad:pad + lc, :] = xbc_ref[0].astype(F32)
    xall = xpad[...]
    conv = cb_ref[...] + xall[pad:pad + lc, :] * cw_ref[CONV_W - 1:CONV_W, :]
    for k in range(1, CONV_W):
        conv = conv + pltpu.roll(xall, k, 0)[pad:pad + lc, :] * cw_ref[CONV_W - 1 - k:CONV_W - k, :]
    tail = xpad[pad + lc - (CONV_W - 1):pad + lc, :]
    xpad[pad - (CONV_W - 1):pad, :] = tail
    act = _silu(conv)
    xs = act[:, :s_dim]
    bm = act[:, s_dim:s_dim + n_groups * n_state]
    cm = act[:, s_dim + n_groups * n_state:]

    dt_col = _softplus(sm_ref[0][:, dt_off:dt_off + n_sh])
    dt_row = _softplus(smt_ref[0][dt_off:dt_off + n_sh, :])
    la_col = dt_col * (-jnp.exp(alr_ref[...]))
    la_row = dt_row * (-jnp.exp(alc_ref[...]))
    row = lax.broadcasted_iota(jnp.int32, (lc, lc), 0)
    col = lax.broadcasted_iota(jnp.int32, (lc, lc), 1)
    causal = col <= row
    tri = causal.astype(BF16)
    triu = (row <= col).astype(BF16)
    cum_col = sum(jnp.dot(tri, p, preferred_element_type=F32) for p in _split3(la_col))
    cum_row = sum(jnp.dot(p, triu, preferred_element_type=F32) for p in _split3(la_row))
    w_end = jnp.exp(cum_col[lc - 1:lc, :] - cum_col) * dt_col
    ex3 = ex_ref[...]
    ecum_x = jnp.dot(jnp.concatenate(_split3(jnp.exp(cum_col)), axis=1), ex3, preferred_element_type=F32)
    wend_x = jnp.dot(jnp.concatenate(_split3(w_end), axis=1), ex3, preferred_element_type=F32)
    xw = (xs * wend_x).astype(BF16)
    xs_b = xs.astype(BF16)
    head_of_lane = lax.div(lax.broadcasted_iota(jnp.int32, (1, gw), 1), pdim)

    parts = []
    for g in range(n_groups):
        bm_g = bm[:, g * n_state:(g + 1) * n_state]
        cm_b = cm[:, g * n_state:(g + 1) * n_state].astype(BF16)
        cb = lax.dot_general(cm_b, bm_g.astype(BF16), NT_DIMS, preferred_element_type=F32)
        xg = xs_b[:, g * gw:(g + 1) * gw]
        yg = (jnp.dot(cm_b, ht_sc[g].astype(BF16), preferred_element_type=F32)
              * ecum_x[:, g * gw:(g + 1) * gw])
        for kk in range(hpg):
            hd = g * hpg + kk
            dec = jnp.exp(jnp.where(causal, cum_col[:, hd:hd + 1] - cum_row[hd:hd + 1, :], -jnp.inf))
            wm = (cb * dec * dt_row[hd:hd + 1, :]).astype(BF16)
            xm = jnp.where(head_of_lane == kk, xg, jnp.zeros_like(xg))
            yg = yg + jnp.dot(wm, xm, preferred_element_type=F32)
        ht_sc[g] = (ecum_x[lc - 1:lc, g * gw:(g + 1) * gw] * ht_sc[g]
                    + jnp.dot(bm_g.T.astype(BF16), xw[:, g * gw:(g + 1) * gw], preferred_element_type=F32))
        parts.append(yg)
    y = jnp.concatenate(parts, axis=1) + dsk_ref[...] * xs
    zf = z_ref[0].astype(F32)
    y = y * _silu(zf)
    ys_ref[0] = _rms(y, nw_ref[...]).astype(BF16)

    @pl.when(ci == pl.num_programs(1) - 1)
    def _():
        conv_ref[0] = tail
        for g in range(n_groups):
            ssm_ref[0, g] = ht_sc[g].T


def _ssd(big3, small3, small_t, conv0, ssm0, conv_w, conv_b, a_log, d_skip, norm_w, *, m_dim, n_mh):
    b, s, _ = big3.shape
    _, n_sh, pdim, n_state = ssm0.shape
    cd = conv0.shape[2]
    s_dim = n_sh * pdim
    n_groups = (cd - s_dim) // (2 * n_state)
    gw = s_dim // n_groups
    lc = _pick(s, (256, 128, 64, 32, 16))
    xbc_blk = (4 * m_dim) // cd
    z_blk = (4 * m_dim + cd) // s_dim
    assert xbc_blk * cd == 4 * m_dim and z_blk * s_dim == 4 * m_dim + cd
    kern = functools.partial(_ssd_kernel, dt_off=2 * n_mh, n_sh=n_sh, n_groups=n_groups,
                             n_state=n_state, s_dim=s_dim)
    full2 = lambda r, c: pl.BlockSpec((r, c), lambda bi, ci: (0, 0))
    expander = jnp.tile(jnp.repeat(jnp.eye(n_sh, dtype=BF16), pdim, axis=1), (3, 1))
    ys, ssm, conv = pl.pallas_call(
        kern,
        grid=(b, s // lc),
        in_specs=[pl.BlockSpec((1, lc, cd), lambda bi, ci: (bi, ci, xbc_blk)),
                  pl.BlockSpec((1, lc, s_dim), lambda bi, ci: (bi, ci, z_blk)),
                  pl.BlockSpec((1, lc, SMALL_W), lambda bi, ci: (bi, ci, 0)),
                  pl.BlockSpec((1, SMALL_W, lc), lambda bi, ci: (bi, 0, ci)),
                  pl.BlockSpec((1, CONV_W - 1, cd), lambda bi, ci: (bi, 0, 0)),
                  pl.BlockSpec((1, n_groups, gw, n_state), lambda bi, ci: (bi, 0, 0, 0)),
                  full2(CONV_W, cd), full2(1, cd), full2(1, n_sh), full2(n_sh, 1),
                  full2(1, s_dim), full2(1, s_dim), full2(3 * n_sh, s_dim)],
        out_specs=[pl.BlockSpec((1, lc, s_dim), lambda bi, ci: (bi, ci, 0)),
                   pl.BlockSpec((1, n_groups, gw, n_state), lambda bi, ci: (bi, 0, 0, 0)),
                   pl.BlockSpec((1, CONV_W - 1, cd), lambda bi, ci: (bi, 0, 0))],
        out_shape=[jax.ShapeDtypeStruct((b, s, s_dim), BF16),
                   jax.ShapeDtypeStruct((b, n_groups, gw, n_state), F32),
                   jax.ShapeDtypeStruct((b, CONV_W - 1, cd), F32)],
        scratch_shapes=[pltpu.VMEM((lc + SUBLANES, cd), F32), pltpu.VMEM((n_groups, n_state, gw), F32)],
        compiler_params=_params(("parallel", "arbitrary")),
        name="ssd",
    )(big3, big3, small3, small_t, conv0, ssm0.reshape(b, n_groups, gw, n_state), conv_w,
      conv_b.reshape(1, cd), a_log.reshape(1, n_sh), a_log.reshape(n_sh, 1),
      jnp.repeat(d_skip, pdim).reshape(1, s_dim), norm_w.reshape(1, s_dim), expander)
    return ys, ssm.reshape(b, n_sh, pdim, n_state), conv


def _out_kernel(x_ref, hm_ref, ys_ref, ga_ref, gb_ref, wa_ref, wb_ref, wo_ref, n2_ref, xo_ref, xn_ref):
    a = jnp.dot(hm_ref[...], wa_ref[...], preferred_element_type=F32)
    bb = jnp.dot(ys_ref[...], wb_ref[...], preferred_element_type=F32)
    mix = _sigmoid(ga_ref[...].astype(F32)) * a + _sigmoid(gb_ref[...].astype(F32)) * bb
    xo = x_ref[...] + jnp.dot(mix.astype(BF16), wo_ref[...], preferred_element_type=F32)
    xo_ref[...] = xo
    xn_ref[...] = _rms(xo, n2_ref[...]).astype(BF16)


def _out_proj(x2d, hm2, ys2, big2, w_a, w_b, w_out, norm2_w, ga_blk):
    t, d = x2d.shape
    tm = _pick(t, (512, 256, 128))
    full = lambda a: pl.BlockSpec(a.shape, lambda i: (0, 0))
    rows = lambda w: pl.BlockSpec((tm, w), lambda i: (i, 0))
    return pl.pallas_call(
        _out_kernel,
        grid=(t // tm,),
        in_specs=[rows(d), rows(hm2.shape[1]), rows(ys2.shape[1]),
                  pl.BlockSpec((tm, d), lambda i: (i, ga_blk)),
                  pl.BlockSpec((tm, d), lambda i: (i, ga_blk + 1)),
                  full(w_a), full(w_b), full(w_out), full(norm2_w)],
        out_specs=[rows(d), rows(d)],
        out_shape=[jax.ShapeDtypeStruct((t, d), F32), jax.ShapeDtypeStruct((t, d), BF16)],
        compiler_params=_params(("parallel",)),
        name="out_proj",
    )(x2d, hm2, ys2, big2, big2, w_a, w_b, w_out, norm2_w)


N_TOP = PEER_TOPK + 1
GELU_C1 = (2.0 / 3.141592653589793) ** 0.5
GELU_C2 = GELU_C1 * 0.044715


def _sort_network(n):
    def merge(lo, hi, r):
        step = 2 * r
        if step < hi - lo:
            yield from merge(lo, hi, step)
            yield from merge(lo + r, hi, step)
            yield from ((i, i + r) for i in range(lo + r, hi - r, step))
        else:
            yield (lo, lo + r)

    def sort(lo, hi):
        if hi > lo:
            mid = lo + (hi - lo) // 2
            yield from sort(lo, mid)
            yield from sort(mid + 1, hi)
            yield from merge(lo, hi, 1)

    return tuple(sort(0, n - 1))


def _all_sublanes(op, x):
    shift = SUBLANES // 2
    while shift:
        x = op(x, pltpu.roll(x, shift, 0))
        shift //= 2
    return x


def _merge_columns(v):
    n = len(v)
    dropped = None
    shift = SUBLANES // 2
    while shift:
        partner = [pltpu.roll(v[n - 1 - k], shift, 0) for k in range(n)]
        lo = functools.reduce(jnp.maximum, [jnp.minimum(v[k], partner[k]) for k in range(n)])
        dropped = lo if dropped is None else jnp.maximum(dropped, lo)
        v = [jnp.maximum(v[k], partner[k]) for k in range(n)]
        stride = n // 2
        while stride:
            for i in range(n):
                if not i & stride:
                    v[i], v[i + stride] = jnp.maximum(v[i], v[i + stride]), jnp.minimum(v[i], v[i + stride])
            stride //= 2
        shift //= 2
    return v, dropped


def _top_values(tiles):
    v = list(tiles)
    assert len(v) == PEER_TOPK
    for i, j in _sort_network(len(v)):
        v[i], v[j] = jnp.maximum(v[i], v[j]), jnp.minimum(v[i], v[j])
    top, dropped = _merge_columns(v)
    return top + [_all_sublanes(jnp.maximum, dropped)]


def _pair_threshold(a, b):
    sub = lax.broadcasted_iota(jnp.int32, (SUBLANES, LANES), 0)
    b0 = b[SUBLANES - 1]
    for s in reversed(range(SUBLANES - 1)):
        b0 = jnp.where(sub == s, b[s], b0)
    top, dropped = _merge_columns([b0 + a[k] for k in range(PEER_TOPK)])
    t17 = _all_sublanes(jnp.maximum, dropped)
    x = a[PEER_TOPK] + b[0]
    extra = []
    for l in range(SUBLANES, N_TOP):
        y = a[0] + b[l]
        extra.append(jnp.maximum(y, x))
        x = jnp.minimum(y, x)
    extra.append(x)
    sel = list(top)
    for j, e in enumerate(extra):
        k = PEER_TOPK - 1 - j
        t17 = jnp.maximum(t17, jnp.minimum(top[k], e))
        sel[k] = jnp.maximum(top[k], e)
    v16 = functools.reduce(jnp.minimum, sel[PEER_TOPK - 1 - len(extra):])
    return sel, v16, t17


def _peer_kernel(xn_ref, x_ref, wq_ref, keys_ref, u_ref, unext_ref, vt_ref, fw_ref, out_ref,
                 tau_sc, e0_sc, s1_sc, e1_sc, s0_sc, acc_sc,
                 sa_sc, sb_sc, wa_sc, wb_sc, vprev_sc, trow_sc, erow_sc, qt_sc, *, n_ph, final):
    et = pl.program_id(1)
    tb = xn_ref.shape[0]
    nk = keys_ref.shape[1]
    phalf = keys_ref.shape[2]
    e_tile = u_ref.shape[0]

    @pl.when(et == 0)
    def _():
        acc_sc[...] = jnp.zeros_like(acc_sc)
        wb_sc[...] = jnp.zeros_like(wb_sc)
        vprev_sc[...] = jnp.zeros_like(vprev_sc)
        xn = xn_ref[...]
        q_rows = wq_ref.shape[0] // 2
        for c in range(2):
            rows = slice(c * q_rows, (c + 1) * q_rows)
            qt_sc[rows, :] = lax.dot_general(wq_ref[rows, :], xn, NT_DIMS,
                                             preferred_element_type=F32).astype(BF16)

        def head(h, carry):
            for c, dst in ((0, s0_sc), (1, s1_sc)):
                hc = 2 * h + c
                q_t = qt_sc[pl.ds(pl.multiple_of(hc * phalf, phalf), phalf), :]
                dst[...] = jnp.dot(keys_ref[hc], q_t, preferred_element_type=F32)
            for lc in range(tb // LANES):
                lanes = slice(lc * LANES, (lc + 1) * LANES)
                rows = [slice(k * SUBLANES, (k + 1) * SUBLANES) for k in range(nk // SUBLANES)]
                s0 = [s0_sc[r, lanes] for r in rows]
                s1 = [s1_sc[r, lanes] for r in rows]
                a = _top_values(s0)
                b = _top_values(s1)
                sel, v16, v17 = _pair_threshold(a, b)
                tau = 0.5 * (v16 + v17)
                z = functools.reduce(jnp.add, [jnp.exp(t - sel[0]) for t in sel])
                rz = 0.5 / z
                tb1 = tau - b[0]
                for r, t0, t1 in zip(rows, s0, s1):
                    tau_sc[h, r, lanes] = jnp.exp(tb1 - t0)
                    e0_sc[h, r, lanes] = jnp.exp(t0 - a[0]) * rz
                    e1_sc[h, r, lanes] = jnp.exp(t1 - b[0])
            return carry

        lax.fori_loop(0, n_ph, head, 0)
        sa_sc[...] = lax.dot_general(u_ref[0:e_tile // 2, :], xn, NT_DIMS, preferred_element_type=F32)

    half = e_tile // 2

    base = pl.multiple_of(et * 2 * SUBLANES, 2 * SUBLANES)
    for h in range(n_ph):
        trow_sc[h] = tau_sc[h, pl.ds(base, 2 * SUBLANES), :]
        erow_sc[h] = e0_sc[h, pl.ds(base, 2 * SUBLANES), :]

    def gates(tile, s_in, w_out, lc, anchor=None):
        rows = slice(tile * SUBLANES, (tile + 1) * SUBLANES)
        lanes = slice(lc * LANES, (lc + 1) * LANES)
        n_jt = nk // SUBLANES
        last = None
        for r in range(SUBLANES):
            g = [None] * n_jt
            for h in range(n_ph):
                t0 = jnp.broadcast_to(trow_sc[h, rows, lanes][r:r + 1], (SUBLANES, LANES))
                ee = jnp.broadcast_to(erow_sc[h, rows, lanes][r:r + 1], (SUBLANES, LANES))
                for jt in range(n_jt):
                    x1 = e1_sc[h, jt * SUBLANES:(jt + 1) * SUBLANES, lanes]
                    term = jnp.where(x1 >= t0, x1, 0.0) * ee
                    g[jt] = term if g[jt] is None else g[jt] + term
            for jp in range(n_jt // 2):
                r0 = r * nk + jp * 2 * SUBLANES
                w_pair = []
                for jj in range(2):
                    sv = s_in[r0 + jj * SUBLANES:r0 + (jj + 1) * SUBLANES, lanes]
                    if anchor is not None:
                        sv, anchor = sv + anchor, None
                    inner = sv * (GELU_C1 + GELU_C2 * (sv * sv))
                    w_tile = g[2 * jp + jj] * (sv + sv * jnp.tanh(inner))
                    w_pair.append(w_tile)
                    last = w_tile if last is None else jnp.maximum(last, w_tile)
                w_out[r0:r0 + 2 * SUBLANES, lanes] = jnp.concatenate(w_pair, axis=0).astype(BF16)
        return last

    def held(operand, after):
        if after is None:
            return operand
        return operand + _exact_zero(after)[0:1, 0:1].astype(operand.dtype)

    def acc_piece(v_ref, w_ref, after):
        res = acc_sc[...] + jnp.dot(v_ref[...], held(w_ref[...], after), preferred_element_type=F32)
        acc_sc[...] = res
        return res[-SUBLANES:, -LANES:]

    def score_piece(s_ref, rows_ref, after):
        res = lax.dot_general(rows_ref[...], held(xn_ref[...], after), NT_DIMS,
                              preferred_element_type=F32)
        s_ref[...] = res
        return res[-SUBLANES:, -LANES:]

    ub_ref = u_ref.at[half:, :]
    va_ref = vt_ref.at[:, 0:half]
    n_lc = tb // LANES
    n_sec = 2 * n_lc
    mxu_work = [(lambda a: acc_piece(vprev_sc, wb_sc, a), 0, n_lc // 2),
                (lambda a: score_piece(sb_sc, ub_ref, a), min(1, n_lc - 1), n_lc),
                (lambda a: acc_piece(va_ref, wa_sc, a), n_lc, n_lc + n_lc // 2),
                (lambda a: score_piece(sa_sc, unext_ref, a), n_lc + 1, n_sec - 1)]
    anchored = n_lc >= 4
    sec_out, due = [], {}
    for sec in range(n_sec):
        tile, lc = divmod(sec, n_lc)
        for fn, at, by in mxu_work:
            if at == sec:
                after = sec_out[sec - 2] if anchored and sec >= 2 else None
                due[by] = fn(after)
        anchor = _exact_zero(due[sec]) if anchored and sec in due and sec > 0 else None
        sec_out.append(gates(tile, (sa_sc, sb_sc)[tile], (wa_sc, wb_sc)[tile], lc, anchor))
    vprev_sc[...] = vt_ref[:, half:]

    @pl.when(et == pl.num_programs(1) - 1)
    def _():
        tail = jnp.dot(vprev_sc[...], wb_sc[...], preferred_element_type=F32)
        res = x_ref[...] + (acc_sc[...] + tail).T
        out_ref[...] = _rms(res, fw_ref[...]) if final else res


def _peer(xn2, x2d, wq_t, keys, u_b, v_t, final_w, final):
    t, d = x2d.shape
    n_hc, nk, phalf = keys.shape
    n_ph = n_hc // 2
    n_exp = u_b.shape[0]
    tb = _pick(t, (512, 256))
    half = SUBLANES * nk
    e_tile = 2 * half
    assert n_exp % e_tile == 0 and tb % MXU_COLS == 0
    n_steps = n_exp // e_tile
    kern = functools.partial(_peer_kernel, n_ph=n_ph, final=final)
    head_buf = lambda: pltpu.VMEM((n_ph, nk, tb), F32)
    once = dict(pipeline_mode=pl.Buffered(1))
    return pl.pallas_call(
        kern,
        grid=(t // tb, n_exp // e_tile),
        in_specs=[pl.BlockSpec((tb, d), lambda i, e: (i, 0), **once),
                  pl.BlockSpec((tb, d), lambda i, e: (i, 0), **once),
                  pl.BlockSpec(wq_t.shape, lambda i, e: (0, 0), **once),
                  pl.BlockSpec(keys.shape, lambda i, e: (0, 0, 0), **once),
                  pl.BlockSpec((e_tile, d), lambda i, e: (e, 0)),
                  pl.BlockSpec((half, d), lambda i, e: (jnp.minimum(2 * e + 2, 2 * n_steps - 2), 0)),
                  pl.BlockSpec((d, e_tile), lambda i, e: (0, e)),
                  pl.BlockSpec((1, d), lambda i, e: (0, 0))],
        out_specs=pl.BlockSpec((tb, d), lambda i, e: (i, 0)),
        out_shape=jax.ShapeDtypeStruct((t, d), F32),
        scratch_shapes=[head_buf(), head_buf(), pltpu.VMEM((nk, tb), F32), head_buf(),
                        pltpu.VMEM((nk, tb), F32),
                        pltpu.VMEM((d, tb), F32),
                        pltpu.VMEM((half, tb), F32), pltpu.VMEM((half, tb), F32),
                        pltpu.VMEM((half, tb), BF16), pltpu.VMEM((half, tb), BF16),
                        pltpu.VMEM((d, half), BF16),
                        pltpu.VMEM((n_ph, 2 * SUBLANES, tb), F32), pltpu.VMEM((n_ph, 2 * SUBLANES, tb), F32),
                        pltpu.VMEM((wq_t.shape[0], tb), BF16)],
        compiler_params=_params(("parallel", "arbitrary")),
        name="peer",
    )(xn2, x2d, wq_t, keys, u_b, u_b, v_t, final_w.reshape(1, d))


def _prep_layer(l, w, dims):
    m_dim, n_mh, s_dim, cd, n_sh, d = dims
    w_in = w["w_in"][l]
    o = [0]
    def take(n):
        o[0] += n
        return w_in[:, o[0] - n:o[0]]
    qkvo, w_ig, w_fg = take(4 * m_dim), take(n_mh), take(n_mh)
    w_z, w_xbc, w_dt, w_g = take(s_dim), take(cd), take(n_sh), take(2 * d)
    n_small = 2 * n_mh + n_sh
    w_small = jnp.concatenate([w_ig, w_fg, w_dt, jnp.zeros((d, SMALL_W - n_small), F32)], axis=1)
    b_small = jnp.concatenate([w["b_igate"][l], w["b_fgate"][l], w["dt_bias"][l],
                               jnp.zeros((SMALL_W - n_small,), F32)]).reshape(1, SMALL_W)
    keys = w["peer_keys"][l]
    return dict(
        norm1=w["norm1_w"][l].reshape(1, d),
        w_big=jnp.concatenate([qkvo, w_xbc, w_z, w_g], axis=1).astype(BF16),
        w_small=w_small.astype(BF16), b_small=b_small,
        w_a=w["w_a"][l].astype(BF16), w_b=w["w_b"][l].astype(BF16), w_out=w["w_out"][l].astype(BF16),
        norm2=w["norm2_w"][l].reshape(1, d),
        wq_t=w["peer_wq"][l].T.astype(BF16),
        keys=keys.reshape((keys.shape[0] * 2,) + keys.shape[2:]).astype(BF16),
        u_b=w["peer_u"][l].astype(BF16), v_t=w["peer_v"][l].T.astype(BF16),
    )


def _trunk(x, c0, n0, m0, ssm0, conv0, w, prepped, dims):
    m_dim, n_mh, s_dim, cd, n_sh, d = dims
    b, s, _ = x.shape
    x2 = x.reshape(b * s, d)
    outs = [[] for _ in range(5)]
    for l, p in enumerate(prepped):
        big2, small2 = _in_proj(x2, p["norm1"], p["w_big"], p["w_small"], p["b_small"])
        big3 = big2.reshape(b, s, -1)
        small3 = small2.reshape(b, s, SMALL_W)
        small_t = small3.transpose(0, 2, 1)
        hm, c, n, m = _mlstm(big3, small_t, c0[l], n0[l], m0[l], w["mlstm_norm_w"][l])
        ys, ssm, conv = _ssd(big3, small3, small_t, conv0[l], ssm0[l], w["conv_w"][l], w["conv_b"][l],
                             w["a_log"][l], w["d_skip"][l], w["ssm_norm_w"][l], m_dim=m_dim, n_mh=n_mh)
        ga_blk = (4 * m_dim + cd + s_dim) // d
        x1, xn2 = _out_proj(x2, hm.reshape(b * s, -1), ys.reshape(b * s, -1), big2,
                            p["w_a"], p["w_b"], p["w_out"], p["norm2"], ga_blk)
        x2 = _peer(xn2, x1, p["wq_t"], p["keys"], p["u_b"], p["v_t"], w["final_norm_w"],
                   final=l == len(prepped) - 1)
        for lst, val in zip(outs, (c, n.reshape(b, n_mh, -1), m.reshape(b, n_mh), ssm, conv)):
            lst.append(val)
    return (x2.reshape(b, s, d),) + tuple(jnp.stack(o) for o in outs)


def kernel(x_prompt, x_sample, state_mlstm_C, state_mlstm_n, state_mlstm_m, state_ssm, cache_conv,
           norm1_w, w_in, b_igate, b_fgate, mlstm_norm_w, conv_w, conv_b, dt_bias, a_log, d_skip,
           ssm_norm_w, w_a, w_b, w_out, norm2_w, peer_wq, peer_keys, peer_u, peer_v, final_norm_w):
    w = dict(norm1_w=norm1_w, w_in=w_in, b_igate=b_igate, b_fgate=b_fgate, mlstm_norm_w=mlstm_norm_w,
             conv_w=conv_w, conv_b=conv_b, dt_bias=dt_bias, a_log=a_log, d_skip=d_skip,
             ssm_norm_w=ssm_norm_w, w_a=w_a, w_b=w_b, w_out=w_out, norm2_w=norm2_w, peer_wq=peer_wq,
             peer_keys=peer_keys, peer_u=peer_u, peer_v=peer_v, final_norm_w=final_norm_w)
    depth, _, n_mh, dh, _ = state_mlstm_C.shape
    n_sh, pdim = state_ssm.shape[2], state_ssm.shape[3]
    d = x_prompt.shape[-1]
    dims = (n_mh * dh, n_mh, n_sh * pdim, cache_conv.shape[-1], n_sh, d)
    prepped = [_prep_layer(l, w, dims) for l in range(depth)]
    bp = x_prompt.shape[0]
    zeros = lambda a: jnp.zeros((depth, bp) + a.shape[2:], F32)
    yp = _trunk(x_prompt, zeros(state_mlstm_C), zeros(state_mlstm_n), zeros(state_mlstm_m),
                zeros(state_ssm), zeros(cache_conv), w, prepped, dims)
    ys = _trunk(x_sample, state_mlstm_C, state_mlstm_n, state_mlstm_m, state_ssm, cache_conv,
                w, prepped, dims)
    return (yp[0], ys[0]) + yp[1:] + ys[1:]
```

```python
import functools

import jax
import jax.numpy as jnp
from jax import lax
from jax.experimental import pallas as pl
from jax.experimental.pallas import tpu as pltpu

F32 = jnp.float32
BF16 = jnp.bfloat16
EPS = 1e-6
CONV_W = 4
PEER_TOPK = 16
LANES = 128
SUBLANES = 8
MXU_COLS = 256
SMALL_W = 128
VMEM_LIMIT = 56 * 1024 * 1024
NT_DIMS = (((1,), (1,)), ((), ()))


def _pick(n, prefs):
    for p in prefs:
        if n % p == 0:
            return p
    return n


def _softplus(x):
    return jnp.maximum(x, 0.0) + jnp.log1p(jnp.exp(-jnp.abs(x)))


def _log_sigmoid(x):
    return jnp.minimum(x, 0.0) - jnp.log1p(jnp.exp(-jnp.abs(x)))


def _sigmoid(x):
    return 0.5 * jnp.tanh(0.5 * x) + 0.5


def _silu(x):
    h = 0.5 * x
    return h + h * jnp.tanh(h)


def _split3(x):
    hi = x.astype(BF16)
    r1 = x - hi.astype(F32)
    mid = r1.astype(BF16)
    lo = (r1 - mid.astype(F32)).astype(BF16)
    return hi, mid, lo


def _exact_zero(tile):
    bits = pltpu.bitcast(tile, jnp.uint32)
    sixteen = jnp.uint32(16)
    return pltpu.bitcast(lax.shift_right_logical(lax.shift_right_logical(bits, sixteen), sixteen), F32)


def _rms(x, w):
    return x * lax.rsqrt(jnp.mean(x * x, axis=-1, keepdims=True) + EPS) * w


def _params(sem):
    return pltpu.CompilerParams(dimension_semantics=sem, vmem_limit_bytes=VMEM_LIMIT)


def _in_proj_kernel(x_ref, nw_ref, wbig_ref, wsmall_ref, bsmall_ref, big_ref, small_ref, h_sc):
    @pl.when(pl.program_id(1) == 0)
    def _():
        hb = _rms(x_ref[...], nw_ref[...]).astype(BF16)
        h_sc[...] = hb
        small_ref[...] = jnp.dot(hb, wsmall_ref[...], preferred_element_type=F32) + bsmall_ref[...]

    big_ref[...] = jnp.dot(h_sc[...], wbig_ref[...], preferred_element_type=F32).astype(BF16)


def _in_proj(x2d, nw, w_big, w_small, b_small):
    t, d = x2d.shape
    nbig = w_big.shape[1]
    tm = _pick(t, (1024, 512, 256, 128))
    tn = _pick(nbig, (2048, 1536, 1024, 512, 256, 128))
    return pl.pallas_call(
        _in_proj_kernel,
        grid=(t // tm, nbig // tn),
        in_specs=[
            pl.BlockSpec((tm, d), lambda i, j: (i, 0)),
            pl.BlockSpec((1, d), lambda i, j: (0, 0)),
            pl.BlockSpec((d, tn), lambda i, j: (0, j)),
            pl.BlockSpec((d, SMALL_W), lambda i, j: (0, 0)),
            pl.BlockSpec((1, SMALL_W), lambda i, j: (0, 0)),
        ],
        out_specs=[
            pl.BlockSpec((tm, tn), lambda i, j: (i, j)),
            pl.BlockSpec((tm, SMALL_W), lambda i, j: (i, 0)),
        ],
        out_shape=[jax.ShapeDtypeStruct((t, nbig), BF16), jax.ShapeDtypeStruct((t, SMALL_W), F32)],
        scratch_shapes=[pltpu.VMEM((tm, d), BF16)],
        compiler_params=_params(("parallel", "arbitrary")),
        name="in_proj",
    )(x2d, nw, w_big, w_small, b_small)


def _mlstm_kernel(q_ref, k_ref, v_ref, og_ref, g_ref, c0_ref, n0_ref, m0_ref, nw_ref,
                  h_ref, c_ref, n_ref, m_ref, c_sc, n_sc, m_sc, *, n_heads, scale):
    ci = pl.program_id(1)
    lc = q_ref.shape[1]
    dh = c_sc.shape[2]

    @pl.when(ci == 0)
    def _():
        c_sc[...] = c0_ref[...]
        n_sc[...] = n0_ref[...]
        m_sc[...] = m0_ref[...]

    row = lax.broadcasted_iota(jnp.int32, (lc, lc), 0)
    col = lax.broadcasted_iota(jnp.int32, (lc, lc), 1)
    causal = col <= row
    tri = causal.astype(BF16)
    triu = (row <= col).astype(BF16)
    for bb, hh in [(bb, hh) for bb in range(q_ref.shape[0]) for hh in range(n_heads)]:
        gates = g_ref[bb]
        cols = slice(hh * dh, (hh + 1) * dh)
        ig = gates[hh:hh + 1, :]
        lf = _log_sigmoid(gates[n_heads + hh:n_heads + hh + 1, :])
        lf_parts = _split3(jnp.broadcast_to(lf, (SUBLANES, lc)))
        fcum_col = sum(lax.dot_general(tri, p, NT_DIMS, preferred_element_type=F32)
                       for p in lf_parts)[:, 0:1]
        fcum_row = sum(jnp.dot(p, triu, preferred_element_type=F32) for p in lf_parts)[0:1, :]

        m_prev = m_sc[bb, hh]
        logw = jnp.where(causal, fcum_col - fcum_row + ig, -jnp.inf)
        log_prev = fcum_col + m_prev
        m_t = jnp.maximum(log_prev, jnp.max(logw, axis=1, keepdims=True))
        a_prev = jnp.exp(log_prev - m_t)
        decay = jnp.exp(logw - m_t)

        q = q_ref[bb, :, cols]
        k = k_ref[bb, :, cols]
        v = v_ref[bb, :, cols]
        s = lax.dot_general(q, k, NT_DIMS, preferred_element_type=F32) * scale * decay
        num = (jnp.dot(s.astype(BF16), v, preferred_element_type=F32)
               + a_prev * jnp.dot(q, c_sc[bb, hh].astype(BF16), preferred_element_type=F32))
        qn = jnp.sum(q.astype(F32) * n_sc[bb, hh], axis=1, keepdims=True)
        den = jnp.sum(s, axis=1, keepdims=True) + a_prev * qn
        hout = num / jnp.maximum(jnp.abs(den), jnp.exp(-m_t))
        y = _rms(hout, nw_ref[hh]) * _sigmoid(og_ref[bb, :, cols].astype(F32))
        h_ref[bb, :, cols] = y.astype(BF16)

        f_end = fcum_row[:, lc - 1:lc]
        m_new = m_t[lc - 1:lc, :]
        w_row = jnp.exp(f_end - fcum_row + ig - m_new) * scale
        a_end = jnp.exp(f_end + m_prev - m_new)
        kw = (k.astype(F32).T * w_row).astype(BF16)
        c_sc[bb, hh] = a_end * c_sc[bb, hh] + jnp.dot(kw, v, preferred_element_type=F32)
        w8 = jnp.broadcast_to(w_row, (SUBLANES, lc)).astype(BF16)
        n_sc[bb, hh] = a_end * n_sc[bb, hh] + jnp.dot(w8, k, preferred_element_type=F32)[0:1, :]
        m_sc[bb, hh] = m_new

    @pl.when(ci == pl.num_programs(1) - 1)
    def _():
        c_ref[...] = c_sc[...]
        n_ref[...] = n_sc[...]
        m_ref[...] = m_sc[...]


def _mlstm(big3, small_t, c0, n0, m0, norm_w):
    b, s, _ = big3.shape
    _, nh, dh, _ = c0.shape
    m_dim = nh * dh
    lc = _pick(s, (256, 128, 64, 32, 16))
    bg = _pick(b, (2, 1))
    kern = functools.partial(_mlstm_kernel, n_heads=nh, scale=float(dh) ** -0.5)
    blk = lambda off: pl.BlockSpec((bg, lc, m_dim), lambda bi, ci: (bi, ci, off))
    st4 = lambda r, c: pl.BlockSpec((bg, nh, r, c), lambda bi, ci: (bi, 0, 0, 0))
    return pl.pallas_call(
        kern,
        grid=(b // bg, s // lc),
        in_specs=[blk(0), blk(1), blk(2), blk(3),
                  pl.BlockSpec((bg, SUBLANES, lc), lambda bi, ci: (bi, 0, ci)),
                  st4(dh, dh), st4(1, dh), st4(1, 1),
                  pl.BlockSpec((nh, 1, dh), lambda bi, ci: (0, 0, 0))],
        out_specs=[pl.BlockSpec((bg, lc, m_dim), lambda bi, ci: (bi, ci, 0)),
                   st4(dh, dh), st4(1, dh), st4(1, 1)],
        out_shape=[jax.ShapeDtypeStruct((b, s, m_dim), BF16),
                   jax.ShapeDtypeStruct((b, nh, dh, dh), F32),
                   jax.ShapeDtypeStruct((b, nh, 1, dh), F32),
                   jax.ShapeDtypeStruct((b, nh, 1, 1), F32)],
        scratch_shapes=[pltpu.VMEM((bg, nh, dh, dh), F32), pltpu.VMEM((bg, nh, 1, dh), F32),
                        pltpu.VMEM((bg, nh, 1, 1), F32)],
        compiler_params=_params(("parallel", "arbitrary")),
        name="mlstm",
    )(big3, big3, big3, big3, small_t, c0, n0.reshape(b, nh, 1, dh), m0.reshape(b, nh, 1, 1),
      norm_w.reshape(nh, 1, dh))


def _ssd_kernel(xbc_ref, z_ref, sm_ref, smt_ref, conv0_ref, ssm0_ref, cw_ref, cb_ref,
                alr_ref, alc_ref, dsk_ref, nw_ref, ex_ref,
                ys_ref, ssm_ref, conv_ref, xpad, ht_sc, act_sc,
                *, dt_off, n_sh, n_groups, n_state, s_dim):
    ci = pl.program_id(1)
    lc = xbc_ref.shape[1]
    gw = s_dim // n_groups
    hpg = n_sh // n_groups
    pdim = gw // hpg
    pad = SUBLANES

    @pl.when(ci == 0)
    def _():
        xpad[0:pad, :] = jnp.zeros((pad, xpad.shape[1]), F32)
        xpad[pad - (CONV_W - 1):pad, :] = conv0_ref[0]
        for g in range(n_groups):
            ht_sc[g] = ssm0_ref[0, g].T

    xpad[pad:pad + lc, :] = xbc_ref[0].astype(F32)
    cd = xpad.shape[1]
    strip = min(cd, 4 * LANES)
    for c0 in range(0, cd, strip):
        cs = slice(c0, c0 + strip)
        xall = xpad[:, cs]
        conv = cb_ref[:, cs] + xall[pad:pad + lc, :] * cw_ref[CONV_W - 1:CONV_W, cs]
        for k in range(1, CONV_W):
            conv = conv + pltpu.roll(xall, k, 0)[pad:pad + lc, :] * cw_ref[CONV_W - 1 - k:CONV_W - k, cs]
        act_sc[:, cs] = _silu(conv)
    tail = xpad[pad + lc - (CONV_W - 1):pad + lc, :]
    xpad[pad - (CONV_W - 1):pad, :] = tail
    xs = act_sc[:, :s_dim]
    bm = act_sc[:, s_dim:s_dim + n_groups * n_state]
    cm = act_sc[:, s_dim + n_groups * n_state:]

    dt_col = _softplus(sm_ref[0][:, dt_off:dt_off + n_sh])
    dt_row = _softplus(smt_ref[0][dt_off:dt_off + n_sh, :])
    la_col = dt_col * (-jnp.exp(alr_ref[...]))
    la_row = dt_row * (-jnp.exp(alc_ref[...]))
    row = lax.broadcasted_iota(jnp.int32, (lc, lc), 0)
    col = lax.broadcasted_iota(jnp.int32, (lc, lc), 1)
    causal = col <= row
    tri = causal.astype(BF16)
    triu = (row <= col).astype(BF16)
    cum_col = sum(jnp.dot(tri, p, preferred_element_type=F32) for p in _split3(la_col))
    cum_row = sum(jnp.dot(p, triu, preferred_element_type=F32) for p in _split3(la_row))
    w_end = jnp.exp(cum_col[lc - 1:lc, :] - cum_col) * dt_col
    ex3 = ex_ref[...]
    ecum_x = jnp.dot(jnp.concatenate(_split3(jnp.exp(cum_col)), axis=1), ex3, preferred_element_type=F32)
    wend_x = jnp.dot(jnp.concatenate(_split3(w_end), axis=1), ex3, preferred_element_type=F32)
    xw = (xs * wend_x).astype(BF16)
    xs_b = xs.astype(BF16)
    head_of_lane = lax.div(lax.broadcasted_iota(jnp.int32, (1, gw), 1), pdim)

    parts = []
    for g in range(n_groups):
        bm_g = bm[:, g * n_state:(g + 1) * n_state]
        cm_b = cm[:, g * n_state:(g + 1) * n_state].astype(BF16)
        cb = lax.dot_general(cm_b, bm_g.astype(BF16), NT_DIMS, preferred_element_type=F32)
        xg = xs_b[:, g * gw:(g + 1) * gw]
        yg = (jnp.dot(cm_b, ht_sc[g].astype(BF16), preferred_element_type=F32)
              * ecum_x[:, g * gw:(g + 1) * gw])
        for kk in range(hpg):
            hd = g * hpg + kk
            dec = jnp.exp(jnp.where(causal, cum_col[:, hd:hd + 1] - cum_row[hd:hd + 1, :], -jnp.inf))
            wm = (cb * dec * dt_row[hd:hd + 1, :]).astype(BF16)
            xm = jnp.where(head_of_lane == kk, xg, jnp.zeros_like(xg))
            yg = yg + jnp.dot(wm, xm, preferred_element_type=F32)
        ht_sc[g] = (ecum_x[lc - 1:lc, g * gw:(g + 1) * gw] * ht_sc[g]
                    + jnp.dot(bm_g.T.astype(BF16), xw[:, g * gw:(g + 1) * gw], preferred_element_type=F32))
        parts.append(yg)
    y = jnp.concatenate(parts, axis=1) + dsk_ref[...] * xs
    zf = z_ref[0].astype(F32)
    y = y * _silu(zf)
    ys_ref[0] = _rms(y, nw_ref[...]).astype(BF16)

    @pl.when(ci == pl.num_programs(1) - 1)
    def _():
        conv_ref[0] = tail
        for g in range(n_groups):
            ssm_ref[0, g] = ht_sc[g].T


def _ssd(big3, small3, small_t, conv0, ssm0, conv_w, conv_b, a_log, d_skip, norm_w, *, m_dim, n_mh):
    b, s, _ = big3.shape
    _, n_sh, pdim, n_state = ssm0.shape
    cd = conv0.shape[2]
    s_dim = n_sh * pdim
    n_groups = (cd - s_dim) // (2 * n_state)
    gw = s_dim // n_groups
    lc = _pick(s, (256, 128, 64, 32, 16))
    xbc_blk = (4 * m_dim) // cd
    z_blk = (4 * m_dim + cd) // s_dim
    assert xbc_blk * cd == 4 * m_dim and z_blk * s_dim == 4 * m_dim + cd
    kern = functools.partial(_ssd_kernel, dt_off=2 * n_mh, n_sh=n_sh, n_groups=n_groups,
                             n_state=n_state, s_dim=s_dim)
    full2 = lambda r, c: pl.BlockSpec((r, c), lambda bi, ci: (0, 0))
    expander = jnp.tile(jnp.repeat(jnp.eye(n_sh, dtype=BF16), pdim, axis=1), (3, 1))
    ys, ssm, conv = pl.pallas_call(
        kern,
        grid=(b, s // lc),
        in_specs=[pl.BlockSpec((1, lc, cd), lambda bi, ci: (bi, ci, xbc_blk)),
                  pl.BlockSpec((1, lc, s_dim), lambda bi, ci: (bi, ci, z_blk)),
                  pl.BlockSpec((1, lc, SMALL_W), lambda bi, ci: (bi, ci, 0)),
                  pl.BlockSpec((1, SMALL_W, lc), lambda bi, ci: (bi, 0, ci)),
                  pl.BlockSpec((1, CONV_W - 1, cd), lambda bi, ci: (bi, 0, 0)),
                  pl.BlockSpec((1, n_groups, gw, n_state), lambda bi, ci: (bi, 0, 0, 0)),
                  full2(CONV_W, cd), full2(1, cd), full2(1, n_sh), full2(n_sh, 1),
                  full2(1, s_dim), full2(1, s_dim), full2(3 * n_sh, s_dim)],
        out_specs=[pl.BlockSpec((1, lc, s_dim), lambda bi, ci: (bi, ci, 0)),
                   pl.BlockSpec((1, n_groups, gw, n_state), lambda bi, ci: (bi, 0, 0, 0)),
                   pl.BlockSpec((1, CONV_W - 1, cd), lambda bi, ci: (bi, 0, 0))],
        out_shape=[jax.ShapeDtypeStruct((b, s, s_dim), BF16),
                   jax.ShapeDtypeStruct((b, n_groups, gw, n_state), F32),
                   jax.ShapeDtypeStruct((b, CONV_W - 1, cd), F32)],
        scratch_shapes=[pltpu.VMEM((lc + SUBLANES, cd), F32), pltpu.VMEM((n_groups, n_state, gw), F32),
                        pltpu.VMEM((lc, cd), F32)],
        compiler_params=_params(("parallel", "arbitrary")),
        name="ssd",
    )(big3, big3, small3, small_t, conv0, ssm0.reshape(b, n_groups, gw, n_state), conv_w,
      conv_b.reshape(1, cd), a_log.reshape(1, n_sh), a_log.reshape(n_sh, 1),
      jnp.repeat(d_skip, pdim).reshape(1, s_dim), norm_w.reshape(1, s_dim), expander)
    return ys, ssm.reshape(b, n_sh, pdim, n_state), conv


def _out_kernel(x_ref, hm_ref, ys_ref, ga_ref, gb_ref, wa_ref, wb_ref, wo_ref, n2_ref, xo_ref, xn_ref):
    a = jnp.dot(hm_ref[...], wa_ref[...], preferred_element_type=F32)
    bb = jnp.dot(ys_ref[...], wb_ref[...], preferred_element_type=F32)
    mix = _sigmoid(ga_ref[...].astype(F32)) * a + _sigmoid(gb_ref[...].astype(F32)) * bb
    xo = x_ref[...] + jnp.dot(mix.astype(BF16), wo_ref[...], preferred_element_type=F32)
    xo_ref[...] = xo
    xn_ref[...] = _rms(xo, n2_ref[...]).astype(BF16)


def _out_proj(x2d, hm2, ys2, big2, w_a, w_b, w_out, norm2_w, ga_blk):
    t, d = x2d.shape
    tm = _pick(t, (512, 256, 128))
    full = lambda a: pl.BlockSpec(a.shape, lambda i: (0, 0))
    rows = lambda w: pl.BlockSpec((tm, w), lambda i: (i, 0))
    return pl.pallas_call(
        _out_kernel,
        grid=(t // tm,),
        in_specs=[rows(d), rows(hm2.shape[1]), rows(ys2.shape[1]),
                  pl.BlockSpec((tm, d), lambda i: (i, ga_blk)),
                  pl.BlockSpec((tm, d), lambda i: (i, ga_blk + 1)),
                  full(w_a), full(w_b), full(w_out), full(norm2_w)],
        out_specs=[rows(d), rows(d)],
        out_shape=[jax.ShapeDtypeStruct((t, d), F32), jax.ShapeDtypeStruct((t, d), BF16)],
        compiler_params=_params(("parallel",)),
        name="out_proj",
    )(x2d, hm2, ys2, big2, big2, w_a, w_b, w_out, norm2_w)


N_TOP = PEER_TOPK + 1
GELU_C1 = (2.0 / 3.141592653589793) ** 0.5
GELU_C2 = GELU_C1 * 0.044715


def _sort_network(n):
    def merge(lo, hi, r):
        step = 2 * r
        if step < hi - lo:
            yield from merge(lo, hi, step)
            yield from merge(lo + r, hi, step)
            yield from ((i, i + r) for i in range(lo + r, hi - r, step))
        else:
            yield (lo, lo + r)

    def sort(lo, hi):
        if hi > lo:
            mid = lo + (hi - lo) // 2
            yield from sort(lo, mid)
            yield from sort(mid + 1, hi)
            yield from merge(lo, hi, 1)

    return tuple(sort(0, n - 1))


def _all_sublanes(op, x):
    shift = SUBLANES // 2
    while shift:
        x = op(x, pltpu.roll(x, shift, 0))
        shift //= 2
    return x


def _merge_columns(v):
    n = len(v)
    dropped = None
    shift = SUBLANES // 2
    while shift:
        partner = [pltpu.roll(v[n - 1 - k], shift, 0) for k in range(n)]
        lo = functools.reduce(jnp.maximum, [jnp.minimum(v[k], partner[k]) for k in range(n)])
        dropped = lo if dropped is None else jnp.maximum(dropped, lo)
        v = [jnp.maximum(v[k], partner[k]) for k in range(n)]
        stride = n // 2
        while stride:
            for i in range(n):
                if not i & stride:
                    v[i], v[i + stride] = jnp.maximum(v[i], v[i + stride]), jnp.minimum(v[i], v[i + stride])
            stride //= 2
        shift //= 2
    return v, dropped


def _top_values(tiles):
    v = list(tiles)
    assert len(v) == PEER_TOPK
    for i, j in _sort_network(len(v)):
        v[i], v[j] = jnp.maximum(v[i], v[j]), jnp.minimum(v[i], v[j])
    top, dropped = _merge_columns(v)
    return top + [_all_sublanes(jnp.maximum, dropped)]


def _pair_threshold(a, b):
    sub = lax.broadcasted_iota(jnp.int32, (SUBLANES, LANES), 0)
    b0 = b[SUBLANES - 1]
    for s in reversed(range(SUBLANES - 1)):
        b0 = jnp.where(sub == s, b[s], b0)
    top, dropped = _merge_columns([b0 + a[k] for k in range(PEER_TOPK)])
    t17 = _all_sublanes(jnp.maximum, dropped)
    x = a[PEER_TOPK] + b[0]
    extra = []
    for l in range(SUBLANES, N_TOP):
        y = a[0] + b[l]
        extra.append(jnp.maximum(y, x))
        x = jnp.minimum(y, x)
    extra.append(x)
    sel = list(top)
    for j, e in enumerate(extra):
        k = PEER_TOPK - 1 - j
        t17 = jnp.maximum(t17, jnp.minimum(top[k], e))
        sel[k] = jnp.maximum(top[k], e)
    v16 = functools.reduce(jnp.minimum, sel[PEER_TOPK - 1 - len(extra):])
    return sel, v16, t17


def _peer_kernel(xn_ref, x_ref, wq_ref, keys_ref, u_ref, unext_ref, vt_ref, fw_ref, out_ref,
                 tau_sc, e0_sc, s1_sc, e1_sc, s0_sc, acc_sc,
                 sa_sc, sb_sc, wa_sc, wb_sc, vprev_sc, trow_sc, erow_sc, qt_sc, *, n_ph, final):
    et = pl.program_id(1)
    tb = xn_ref.shape[0]
    nk = keys_ref.shape[1]
    phalf = keys_ref.shape[2]
    e_tile = u_ref.shape[0]

    @pl.when(et == 0)
    def _():
        acc_sc[...] = jnp.zeros_like(acc_sc)
        wb_sc[...] = jnp.zeros_like(wb_sc)
        vprev_sc[...] = jnp.zeros_like(vprev_sc)
        xn = xn_ref[...]
        q_rows = wq_ref.shape[0] // 2
        for c in range(2):
            rows = slice(c * q_rows, (c + 1) * q_rows)
            qt_sc[rows, :] = lax.dot_general(wq_ref[rows, :], xn, NT_DIMS,
                                             preferred_element_type=F32).astype(BF16)

        def head(h, carry):
            for c, dst in ((0, s0_sc), (1, s1_sc)):
                hc = 2 * h + c
                q_t = qt_sc[pl.ds(pl.multiple_of(hc * phalf, phalf), phalf), :]
                dst[...] = jnp.dot(keys_ref[hc], q_t, preferred_element_type=F32)
            for lc in range(tb // LANES):
                lanes = slice(lc * LANES, (lc + 1) * LANES)
                rows = [slice(k * SUBLANES, (k + 1) * SUBLANES) for k in range(nk // SUBLANES)]
                s0 = [s0_sc[r, lanes] for r in rows]
                s1 = [s1_sc[r, lanes] for r in rows]
                a = _top_values(s0)
                b = _top_values(s1)
                sel, v16, v17 = _pair_threshold(a, b)
                tau = 0.5 * (v16 + v17)
                z = functools.reduce(jnp.add, [jnp.exp(t - sel[0]) for t in sel])
                rz = 0.5 / z
                tb1 = tau - b[0]
                for r, t0, t1 in zip(rows, s0, s1):
                    tau_sc[h, r, lanes] = jnp.exp(tb1 - t0)
                    e0_sc[h, r, lanes] = jnp.exp(t0 - a[0]) * rz
                    e1_sc[h, r, lanes] = jnp.exp(t1 - b[0])
            return carry

        lax.fori_loop(0, n_ph, head, 0)
        sa_sc[...] = lax.dot_general(u_ref[0:e_tile // 2, :], xn, NT_DIMS, preferred_element_type=F32)

    half = e_tile // 2

    base = pl.multiple_of(et * 2 * SUBLANES, 2 * SUBLANES)
    for h in range(n_ph):
        trow_sc[h] = tau_sc[h, pl.ds(base, 2 * SUBLANES), :]
        erow_sc[h] = e0_sc[h, pl.ds(base, 2 * SUBLANES), :]

    def gates(tile, s_in, w_out, lc, anchor=None):
        rows = slice(tile * SUBLANES, (tile + 1) * SUBLANES)
        lanes = slice(lc * LANES, (lc + 1) * LANES)
        n_jt = nk // SUBLANES
        last = None
        for r in range(SUBLANES):
            g = [None] * n_jt
            for h in range(n_ph):
                t0 = jnp.broadcast_to(trow_sc[h, rows, lanes][r:r + 1], (SUBLANES, LANES))
                ee = jnp.broadcast_to(erow_sc[h, rows, lanes][r:r + 1], (SUBLANES, LANES))
                for jt in range(n_jt):
                    x1 = e1_sc[h, jt * SUBLANES:(jt + 1) * SUBLANES, lanes]
                    term = jnp.where(x1 >= t0, x1, 0.0) * ee
                    g[jt] = term if g[jt] is None else g[jt] + term
            for jp in range(n_jt // 2):
                r0 = r * nk + jp * 2 * SUBLANES
                w_pair = []
                for jj in range(2):
                    sv = s_in[r0 + jj * SUBLANES:r0 + (jj + 1) * SUBLANES, lanes]
                    if anchor is not None:
                        sv, anchor = sv + anchor, None
                    inner = sv * (GELU_C1 + GELU_C2 * (sv * sv))
                    w_tile = g[2 * jp + jj] * (sv + sv * jnp.tanh(inner))
                    w_pair.append(w_tile)
                    last = w_tile if last is None else jnp.maximum(last, w_tile)
                w_out[r0:r0 + 2 * SUBLANES, lanes] = jnp.concatenate(w_pair, axis=0).astype(BF16)
        return last

    def held(operand, after):
        if after is None:
            return operand
        return operand + _exact_zero(after)[0:1, 0:1].astype(operand.dtype)

    def acc_piece(v_ref, w_ref, after):
        res = acc_sc[...] + jnp.dot(v_ref[...], held(w_ref[...], after), preferred_element_type=F32)
        acc_sc[...] = res
        return res[-SUBLANES:, -LANES:]

    def score_piece(s_ref, rows_ref, after):
        res = lax.dot_general(rows_ref[...], held(xn_ref[...], after), NT_DIMS,
                              preferred_element_type=F32)
        s_ref[...] = res
        return res[-SUBLANES:, -LANES:]

    ub_ref = u_ref.at[half:, :]
    va_ref = vt_ref.at[:, 0:half]
    n_lc = tb // LANES
    n_sec = 2 * n_lc
    mxu_work = [(lambda a: acc_piece(vprev_sc, wb_sc, a), 0, n_lc // 2),
                (lambda a: score_piece(sb_sc, ub_ref, a), min(1, n_lc - 1), n_lc),
                (lambda a: acc_piece(va_ref, wa_sc, a), n_lc, n_lc + n_lc // 2),
                (lambda a: score_piece(sa_sc, unext_ref, a), n_lc + 1, n_sec - 1)]
    anchored = n_lc >= 4
    sec_out, due = [], {}
    for sec in range(n_sec):
        tile, lc = divmod(sec, n_lc)
        for fn, at, by in mxu_work:
            if at == sec:
                after = sec_out[sec - 2] if anchored and sec >= 2 else None
                due[by] = fn(after)
        anchor = _exact_zero(due[sec]) if anchored and sec in due and sec > 0 else None
        sec_out.append(gates(tile, (sa_sc, sb_sc)[tile], (wa_sc, wb_sc)[tile], lc, anchor))
    vprev_sc[...] = vt_ref[:, half:]

    @pl.when(et == pl.num_programs(1) - 1)
    def _():
        tail = jnp.dot(vprev_sc[...], wb_sc[...], preferred_element_type=F32)
        res = x_ref[...] + (acc_sc[...] + tail).T
        out_ref[...] = _rms(res, fw_ref[...]) if final else res


def _peer(xn2, x2d, wq_t, keys, u_b, v_t, final_w, final):
    t, d = x2d.shape
    n_hc, nk, phalf = keys.shape
    n_ph = n_hc // 2
    n_exp = u_b.shape[0]
    tb = _pick(t, (512, 256))
    half = SUBLANES * nk
    e_tile = 2 * half
    assert n_exp % e_tile == 0 and tb % MXU_COLS == 0
    n_steps = n_exp // e_tile
    kern = functools.partial(_peer_kernel, n_ph=n_ph, final=final)
    head_buf = lambda: pltpu.VMEM((n_ph, nk, tb), F32)
    once = dict(pipeline_mode=pl.Buffered(1))
    return pl.pallas_call(
        kern,
        grid=(t // tb, n_exp // e_tile),
        in_specs=[pl.BlockSpec((tb, d), lambda i, e: (i, 0), **once),
                  pl.BlockSpec((tb, d), lambda i, e: (i, 0), **once),
                  pl.BlockSpec(wq_t.shape, lambda i, e: (0, 0), **once),
                  pl.BlockSpec(keys.shape, lambda i, e: (0, 0, 0), **once),
                  pl.BlockSpec((e_tile, d), lambda i, e: (e, 0)),
                  pl.BlockSpec((half, d), lambda i, e: (jnp.minimum(2 * e + 2, 2 * n_steps - 2), 0)),
                  pl.BlockSpec((d, e_tile), lambda i, e: (0, e)),
                  pl.BlockSpec((1, d), lambda i, e: (0, 0))],
        out_specs=pl.BlockSpec((tb, d), lambda i, e: (i, 0)),
        out_shape=jax.ShapeDtypeStruct((t, d), F32),
        scratch_shapes=[head_buf(), head_buf(), pltpu.VMEM((nk, tb), F32), head_buf(),
                        pltpu.VMEM((nk, tb), F32),
                        pltpu.VMEM((d, tb), F32),
                        pltpu.VMEM((half, tb), F32), pltpu.VMEM((half, tb), F32),
                        pltpu.VMEM((half, tb), BF16), pltpu.VMEM((half, tb), BF16),
                        pltpu.VMEM((d, half), BF16),
                        pltpu.VMEM((n_ph, 2 * SUBLANES, tb), F32), pltpu.VMEM((n_ph, 2 * SUBLANES, tb), F32),
                        pltpu.VMEM((wq_t.shape[0], tb), BF16)],
        compiler_params=_params(("parallel", "arbitrary")),
        name="peer",
    )(xn2, x2d, wq_t, keys, u_b, u_b, v_t, final_w.reshape(1, d))


def _prep_layer(l, w, dims):
    m_dim, n_mh, s_dim, cd, n_sh, d = dims
    w_in = w["w_in"][l]
    o = [0]
    def take(n):
        o[0] += n
        return w_in[:, o[0] - n:o[0]]
    qkvo, w_ig, w_fg = take(4 * m_dim), take(n_mh), take(n_mh)
    w_z, w_xbc, w_dt, w_g = take(s_dim), take(cd), take(n_sh), take(2 * d)
    n_small = 2 * n_mh + n_sh
    w_small = jnp.concatenate([w_ig, w_fg, w_dt, jnp.zeros((d, SMALL_W - n_small), F32)], axis=1)
    b_small = jnp.concatenate([w["b_igate"][l], w["b_fgate"][l], w["dt_bias"][l],
                               jnp.zeros((SMALL_W - n_small,), F32)]).reshape(1, SMALL_W)
    keys = w["peer_keys"][l]
    return dict(
        norm1=w["norm1_w"][l].reshape(1, d),
        w_big=jnp.concatenate([qkvo, w_xbc, w_z, w_g], axis=1).astype(BF16),
        w_small=w_small.astype(BF16), b_small=b_small,
        w_a=w["w_a"][l].astype(BF16), w_b=w["w_b"][l].astype(BF16), w_out=w["w_out"][l].astype(BF16),
        norm2=w["norm2_w"][l].reshape(1, d),
        wq_t=w["peer_wq"][l].T.astype(BF16),
        keys=keys.reshape((keys.shape[0] * 2,) + keys.shape[2:]).astype(BF16),
        u_b=w["peer_u"][l].astype(BF16), v_t=w["peer_v"][l].T.astype(BF16),
    )


def _trunk(x, c0, n0, m0, ssm0, conv0, w, prepped, dims):
    m_dim, n_mh, s_dim, cd, n_sh, d = dims
    b, s, _ = x.shape
    x2 = x.reshape(b * s, d)
    outs = [[] for _ in range(5)]
    for l, p in enumerate(prepped):
        big2, small2 = _in_proj(x2, p["norm1"], p["w_big"], p["w_small"], p["b_small"])
        big3 = big2.reshape(b, s, -1)
        small3 = small2.reshape(b, s, SMALL_W)
        small_t = small3.transpose(0, 2, 1)
        hm, c, n, m = _mlstm(big3, small_t, c0[l], n0[l], m0[l], w["mlstm_norm_w"][l])
        ys, ssm, conv = _ssd(big3, small3, small_t, conv0[l], ssm0[l], w["conv_w"][l], w["conv_b"][l],
                             w["a_log"][l], w["d_skip"][l], w["ssm_norm_w"][l], m_dim=m_dim, n_mh=n_mh)
        ga_blk = (4 * m_dim + cd + s_dim) // d
        x1, xn2 = _out_proj(x2, hm.reshape(b * s, -1), ys.reshape(b * s, -1), big2,
                            p["w_a"], p["w_b"], p["w_out"], p["norm2"], ga_blk)
        x2 = _peer(xn2, x1, p["wq_t"], p["keys"], p["u_b"], p["v_t"], w["final_norm_w"],
                   final=l == len(prepped) - 1)
        for lst, val in zip(outs, (c, n.reshape(b, n_mh, -1), m.reshape(b, n_mh), ssm, conv)):
            lst.append(val)
    return (x2.reshape(b, s, d),) + tuple(jnp.stack(o) for o in outs)


def kernel(x_prompt, x_sample, state_mlstm_C, state_mlstm_n, state_mlstm_m, state_ssm, cache_conv,
           norm1_w, w_in, b_igate, b_fgate, mlstm_norm_w, conv_w, conv_b, dt_bias, a_log, d_skip,
           ssm_norm_w, w_a, w_b, w_out, norm2_w, peer_wq, peer_keys, peer_u, peer_v, final_norm_w):
    w = dict(norm1_w=norm1_w, w_in=w_in, b_igate=b_igate, b_fgate=b_fgate, mlstm_norm_w=mlstm_norm_w,
             conv_w=conv_w, conv_b=conv_b, dt_bias=dt_bias, a_log=a_log, d_skip=d_skip,
             ssm_norm_w=ssm_norm_w, w_a=w_a, w_b=w_b, w_out=w_out, norm2_w=norm2_w, peer_wq=peer_wq,
             peer_keys=peer_keys, peer_u=peer_u, peer_v=peer_v, final_norm_w=final_norm_w)
    depth, _, n_mh, dh, _ = state_mlstm_C.shape
    n_sh, pdim = state_ssm.shape[2], state_ssm.shape[3]
    d = x_prompt.shape[-1]
    dims = (n_mh * dh, n_mh, n_sh * pdim, cache_conv.shape[-1], n_sh, d)
    prepped = [_prep_layer(l, w, dims) for l in range(depth)]
    bp = x_prompt.shape[0]
    zeros = lambda a: jnp.zeros((depth, bp) + a.shape[2:], F32)
    yp = _trunk(x_prompt, zeros(state_mlstm_C), zeros(state_mlstm_n), zeros(state_mlstm_m),
                zeros(state_ssm), zeros(cache_conv), w, prepped, dims)
    ys = _trunk(x_sample, state_mlstm_C, state_mlstm_n, state_mlstm_m, state_ssm, cache_conv,
                w, prepped, dims)
    return (yp[0], ys[0]) + yp[1:] + ys[1:]
```
